```python
import math
import jax, jax.numpy as jnp
from jax import lax
import numpy as np

D_MODEL = 2048
BATCH = 1
SEQ = 8192
DEPTH = 4

CHUNK = 64
N_MIXERS = 3
RMS_EPS = 1e-6
ROPE_THETA = 10000.0
D_FF = 4 * D_MODEL

RG_WIDTH = D_MODEL
RG_BLOCKS = 8
RG_BLOCK_W = RG_WIDTH // RG_BLOCKS
RG_CONV = 4
RG_C = 8.0

ATT_HEADS = 16
ATT_KV_HEADS = 4
ATT_HEAD_DIM = D_MODEL // ATT_HEADS
ATT_GROUP = ATT_HEADS // ATT_KV_HEADS
IDX_HEADS = 16
IDX_DIM = 64
TOPK_MAX = 256
Q_BLOCK = 128
DSA_SPLITS = (ATT_HEADS * ATT_HEAD_DIM, ATT_KV_HEADS * ATT_HEAD_DIM, ATT_KV_HEADS * ATT_HEAD_DIM,
              IDX_HEADS * IDX_DIM, IDX_DIM, IDX_HEADS)
DSA_IN_DIM = sum(DSA_SPLITS)

SSD_INNER = 2 * D_MODEL
SSD_HEAD_DIM = 64
SSD_HEADS = SSD_INNER // SSD_HEAD_DIM
SSD_GROUPS = 8
SSD_STATE = 128
SSD_CONV = 4
SSD_BLOCK = CHUNK
SSD_CONV_DIM = SSD_INNER + 2 * SSD_GROUPS * SSD_STATE
SSD_IN_DIM = SSD_INNER + SSD_CONV_DIM + SSD_HEADS

kernel_name = 'hybrid_rglru_dsa_ssd_trunk'


def rms_norm(x, g):
    xf = x.astype(jnp.float32)
    y = xf * lax.rsqrt(jnp.mean(xf * xf, axis=-1, keepdims=True) + RMS_EPS)
    return (y * g.astype(jnp.float32)).astype(x.dtype)


def causal_depthwise_conv(x, w, b):
    k, c = w.shape
    y = lax.conv_general_dilated(x, w.astype(x.dtype)[:, None, :], window_strides=(1,),
                                 padding=[(k - 1, 0)], dimension_numbers=('NWC', 'WIO', 'NWC'),
                                 feature_group_count=c)
    return y + b.astype(x.dtype)


def rope_tables(seq, dim):
    inv = ROPE_THETA ** (-jnp.arange(0, dim, 2, dtype=jnp.float32) / dim)
    ang = jnp.arange(seq, dtype=jnp.float32)[:, None] * inv[None, :]
    return jnp.cos(ang), jnp.sin(ang)


def apply_rope(x, cos, sin):
    d2 = x.shape[-1] // 2
    xf = x.astype(jnp.float32)
    x1, x2 = xf[..., :d2], xf[..., d2:]
    c, s = cos[None, :, None, :], sin[None, :, None, :]
    return jnp.concatenate([x1 * c - x2 * s, x2 * c + x1 * s], axis=-1).astype(x.dtype)


def _lin_combine(left, right):
    a1, b1 = left
    a2, b2 = right
    return a1 * a2, a2 * b1 + b2


def rglru_mixer(h, w_in, conv_w, conv_b, w_a, b_a, w_x, b_x, lam, w_out):
    bsz, seq, _ = h.shape
    f32 = jnp.float32
    gate_branch, x_branch = jnp.split(h @ w_in, 2, axis=-1)
    gate_branch = jax.nn.gelu(gate_branch)
    xc = causal_depthwise_conv(x_branch, conv_w, conv_b)
    xb = xc.reshape(bsz, seq, RG_BLOCKS, RG_BLOCK_W)
    r = jax.nn.sigmoid(jnp.einsum('bshi,hij->bshj', xb, w_a).reshape(bsz, seq, RG_WIDTH).astype(f32) + b_a.astype(f32))
    i = jax.nn.sigmoid(jnp.einsum('bshi,hij->bshj', xb, w_x).reshape(bsz, seq, RG_WIDTH).astype(f32) + b_x.astype(f32))
    log_a = -RG_C * r * jax.nn.softplus(-lam.astype(f32))
    a = jnp.exp(log_a)
    mult = jnp.sqrt(-jnp.expm1(2.0 * log_a))
    bterm = mult * (i * xc.astype(f32))
    _, hseq = lax.associative_scan(_lin_combine, (a, bterm), axis=1)
    y = hseq.astype(h.dtype) * gate_branch
    return y @ w_out


def dsa_mixer(h, w_in, w_out):
    bsz, seq, _ = h.shape
    f32 = jnp.float32
    offs = list(np.cumsum(DSA_SPLITS)[:-1])
    q, k, v, qi, ki, wi = jnp.split(h @ w_in, offs, axis=-1)
    q = q.reshape(bsz, seq, ATT_HEADS, ATT_HEAD_DIM)
    k = k.reshape(bsz, seq, ATT_KV_HEADS, ATT_HEAD_DIM)
    v = v.reshape(bsz, seq, ATT_KV_HEADS, ATT_HEAD_DIM)
    qi = qi.reshape(bsz, seq, IDX_HEADS, IDX_DIM)
    ki = ki.reshape(bsz, seq, 1, IDX_DIM)
    cos, sin = rope_tables(seq, ATT_HEAD_DIM)
    q = apply_rope(q, cos, sin).reshape(bsz, seq, ATT_KV_HEADS, ATT_GROUP, ATT_HEAD_DIM)
    k = apply_rope(k, cos, sin)
    ci, si = rope_tables(seq, IDX_DIM)
    qi = apply_rope(qi, ci, si)
    ki = apply_rope(ki, ci, si)[:, :, 0]
    wi = wi.astype(f32) * (IDX_HEADS ** -0.5 * IDX_DIM ** -0.5)
    top_k = min(TOPK_MAX, seq // 4)
    n_blocks = seq // Q_BLOCK
    key_chunk = jnp.arange(seq) // CHUNK
    scale = ATT_HEAD_DIM ** -0.5

    def to_blocks(t):
        return jnp.moveaxis(t.reshape(bsz, n_blocks, Q_BLOCK, *t.shape[2:]), 1, 0)

    def block_fn(args):
        qb, qib, wib, blk = args
        q_chunk = (blk * Q_BLOCK + jnp.arange(Q_BLOCK)) // CHUNK
        admissible = key_chunk[None, :] <= q_chunk[:, None]
        idx_logits = jnp.einsum('bqhd,bsd->bqhs', qib, ki).astype(f32)
        score = jnp.einsum('bqhs,bqh->bqs', jax.nn.relu(idx_logits), wib)
        score = jnp.where(admissible[None], score, -jnp.inf)
        _, sel = lax.top_k(score, top_k)
        valid = key_chunk[sel] <= q_chunk[None, :, None]
        k_sel = jax.vmap(lambda kk, ii: kk[ii])(k, sel)
        v_sel = jax.vmap(lambda vv, ii: vv[ii])(v, sel)
        logits = jnp.einsum('bqhgd,bqkhd->bqhgk', qb, k_sel).astype(f32) * scale
        logits = jnp.where(valid[:, :, None, None, :], logits, -jnp.inf)
        p = jax.nn.softmax(logits, axis=-1).astype(v.dtype)
        return jnp.einsum('bqhgk,bqkhd->bqhgd', p, v_sel)

    out = lax.map(block_fn, (to_blocks(q), to_blocks(qi), to_blocks(wi), jnp.arange(n_blocks)))
    out = jnp.moveaxis(out, 0, 1).reshape(bsz, seq, ATT_HEADS * ATT_HEAD_DIM)
    return out @ w_out


def gated_group_rmsnorm(y, z, g):
    bsz, seq, width = y.shape
    yz = (y * jax.nn.silu(z.astype(jnp.float32))).reshape(bsz, seq, SSD_GROUPS, width // SSD_GROUPS)
    yz = yz * lax.rsqrt(jnp.mean(yz * yz, axis=-1, keepdims=True) + RMS_EPS)
    return yz.reshape(bsz, seq, width) * g.astype(jnp.float32)


def ssd_mixer(h, w_in, conv_w, conv_b, dt_bias, a_log, d_skip, norm_g, w_out):
    bsz, seq, _ = h.shape
    G, HG, P, N, L = SSD_GROUPS, SSD_HEADS // SSD_GROUPS, SSD_HEAD_DIM, SSD_STATE, SSD_BLOCK
    nc = seq // L
    f32 = jnp.float32
    z, xbc, dt = jnp.split(h @ w_in, [SSD_INNER, SSD_INNER + SSD_CONV_DIM], axis=-1)
    xbc = jax.nn.silu(causal_depthwise_conv(xbc, conv_w, conv_b)).astype(f32)
    xs, b_in, c_in = jnp.split(xbc, [SSD_INNER, SSD_INNER + G * N], axis=-1)
    x = xs.reshape(bsz, nc, L, G, HG, P)
    bm = b_in.reshape(bsz, nc, L, G, N)
    cm = c_in.reshape(bsz, nc, L, G, N)
    dt = jax.nn.softplus(dt.astype(f32) + dt_bias.astype(f32)).reshape(bsz, nc, L, G, HG)
    a = -jnp.exp(a_log.astype(f32)).reshape(G, HG)
    a_cs = jnp.cumsum(dt * a, axis=2)
    xdt = x * dt[..., None]
    causal = jnp.tril(jnp.ones((L, L), dtype=bool))[None, None, :, :, None, None]
    seg = a_cs[:, :, :, None] - a_cs[:, :, None, :]
    decay = jnp.exp(jnp.where(causal, seg, -jnp.inf))
    cb = jnp.einsum('bclgn,bcsgn->bclsg', cm, bm)
    y_diag = jnp.einsum('bclsg,bclsgh,bcsghp->bclghp', cb, decay, xdt)
    decay_to_end = jnp.exp(a_cs[:, :, -1:] - a_cs)
    chunk_states = jnp.einsum('bclgn,bclgh,bclghp->bcghpn', bm, decay_to_end, xdt)
    chunk_decay = jnp.exp(a_cs[:, :, -1])

    def carry_state(state, inp):
        st, dec = inp
        return state * dec[..., None, None] + st, state

    init = jnp.zeros((bsz, G, HG, P, N), f32)
    _, prev = lax.scan(carry_state, init, (jnp.moveaxis(chunk_states, 1, 0), jnp.moveaxis(chunk_decay, 1, 0)))
    prev = jnp.moveaxis(prev, 0, 1)
    y_off = jnp.einsum('bclgn,bcghpn,bclgh->bclghp', cm, prev, jnp.exp(a_cs))
    y = (y_diag + y_off + x * d_skip.astype(f32).reshape(G, HG, 1)).reshape(bsz, seq, SSD_INNER)
    y = gated_group_rmsnorm(y, z, norm_g)
    return y.astype(h.dtype) @ w_out


def squared_relu_mlp(h, w_up, w_down):
    u = jax.nn.relu(h @ w_up)
    return (u * u) @ w_down


def _normal(k, shape, scale):
    return jax.random.normal(k, shape, jnp.float32) * scale


def _gain(k, n):
    return 1.0 + 0.02 * jax.random.normal(k, (n,), jnp.float32)


def setup_inputs(seed: int = 0) -> dict:
    key = jax.random.key(seed)
    keys = iter(jax.random.split(key, 128))
    p = {'x': jax.random.normal(next(keys), (BATCH, SEQ, D_MODEL), jnp.float32)}
    for i in range(DEPTH):
        pre = f'l{i}_'
        p[pre + 'mix_norm'] = _gain(next(keys), D_MODEL)
        kind = i % N_MIXERS
        if kind == 0:
            p[pre + 'rg_in_w'] = _normal(next(keys), (D_MODEL, 2 * RG_WIDTH), D_MODEL ** -0.5)
            p[pre + 'rg_conv_w'] = _normal(next(keys), (RG_CONV, RG_WIDTH), RG_CONV ** -0.5)
            p[pre + 'rg_conv_b'] = _normal(next(keys), (RG_WIDTH,), 0.02)
            p[pre + 'rg_wa'] = _normal(next(keys), (RG_BLOCKS, RG_BLOCK_W, RG_BLOCK_W), RG_BLOCK_W ** -0.5)
            p[pre + 'rg_ba'] = _normal(next(keys), (RG_WIDTH,), 0.02)
            p[pre + 'rg_wx'] = _normal(next(keys), (RG_BLOCKS, RG_BLOCK_W, RG_BLOCK_W), RG_BLOCK_W ** -0.5)
            p[pre + 'rg_bx'] = _normal(next(keys), (RG_WIDTH,), 0.02)
            a0 = jax.random.uniform(next(keys), (RG_WIDTH,), jnp.float32, 0.9, 0.999)
            pr = a0 ** (1.0 / RG_C)
            p[pre + 'rg_lambda'] = jnp.log(pr) - jnp.log1p(-pr)
            p[pre + 'rg_out_w'] = _normal(next(keys), (RG_WIDTH, D_MODEL), RG_WIDTH ** -0.5)
        elif kind == 1:
            p[pre + 'dsa_in_w'] = _normal(next(keys), (D_MODEL, DSA_IN_DIM), D_MODEL ** -0.5)
            p[pre + 'dsa_out_w'] = _normal(next(keys), (ATT_HEADS * ATT_HEAD_DIM, D_MODEL), (ATT_HEADS * ATT_HEAD_DIM) ** -0.5)
        else:
            p[pre + 'ssd_in_w'] = _normal(next(keys), (D_MODEL, SSD_IN_DIM), D_MODEL ** -0.5)
            p[pre + 'ssd_conv_w'] = _normal(next(keys), (SSD_CONV, SSD_CONV_DIM), SSD_CONV ** -0.5)
            p[pre + 'ssd_conv_b'] = _normal(next(keys), (SSD_CONV_DIM,), 0.02)
            u = jax.random.uniform(next(keys), (SSD_HEADS,), jnp.float32)
            dt0 = jnp.exp(u * (math.log(0.1) - math.log(0.001)) + math.log(0.001))
            p[pre + 'ssd_dt_bias'] = dt0 + jnp.log(-jnp.expm1(-dt0))
            p[pre + 'ssd_a_log'] = jnp.log(jax.random.uniform(next(keys), (SSD_HEADS,), jnp.float32, 1.0, 16.0))
            p[pre + 'ssd_d'] = _gain(next(keys), SSD_HEADS)
            p[pre + 'ssd_norm'] = _gain(next(keys), SSD_INNER)
            p[pre + 'ssd_out_w'] = _normal(next(keys), (SSD_INNER, D_MODEL), SSD_INNER ** -0.5)
        p[pre + 'mlp_norm'] = _gain(next(keys), D_MODEL)
        p[pre + 'mlp_up'] = _normal(next(keys), (D_MODEL, D_FF), D_MODEL ** -0.5)
        p[pre + 'mlp_down'] = _normal(next(keys), (D_FF, D_MODEL), D_FF ** -0.5)
    p['final_norm'] = _gain(next(keys), D_MODEL)
    return p


def reference(x,
              l0_mix_norm, l0_rg_in_w, l0_rg_conv_w, l0_rg_conv_b, l0_rg_wa, l0_rg_ba, l0_rg_wx, l0_rg_bx,
              l0_rg_lambda, l0_rg_out_w, l0_mlp_norm, l0_mlp_up, l0_mlp_down,
              l1_mix_norm, l1_dsa_in_w, l1_dsa_out_w, l1_mlp_norm, l1_mlp_up, l1_mlp_down,
              l2_mix_norm, l2_ssd_in_w, l2_ssd_conv_w, l2_ssd_conv_b, l2_ssd_dt_bias, l2_ssd_a_log, l2_ssd_d,
              l2_ssd_norm, l2_ssd_out_w, l2_mlp_norm, l2_mlp_up, l2_mlp_down,
              l3_mix_norm, l3_rg_in_w, l3_rg_conv_w, l3_rg_conv_b, l3_rg_wa, l3_rg_ba, l3_rg_wx, l3_rg_bx,
              l3_rg_lambda, l3_rg_out_w, l3_mlp_norm, l3_mlp_up, l3_mlp_down,
              final_norm):
    layers = [
        (l0_mix_norm, (l0_rg_in_w, l0_rg_conv_w, l0_rg_conv_b, l0_rg_wa, l0_rg_ba, l0_rg_wx, l0_rg_bx,
                       l0_rg_lambda, l0_rg_out_w), l0_mlp_norm, l0_mlp_up, l0_mlp_down),
        (l1_mix_norm, (l1_dsa_in_w, l1_dsa_out_w), l1_mlp_norm, l1_mlp_up, l1_mlp_down),
        (l2_mix_norm, (l2_ssd_in_w, l2_ssd_conv_w, l2_ssd_conv_b, l2_ssd_dt_bias, l2_ssd_a_log, l2_ssd_d,
                       l2_ssd_norm, l2_ssd_out_w), l2_mlp_norm, l2_mlp_up, l2_mlp_down),
        (l3_mix_norm, (l3_rg_in_w, l3_rg_conv_w, l3_rg_conv_b, l3_rg_wa, l3_rg_ba, l3_rg_wx, l3_rg_bx,
                       l3_rg_lambda, l3_rg_out_w), l3_mlp_norm, l3_mlp_up, l3_mlp_down),
    ]
    mixers = (rglru_mixer, dsa_mixer, ssd_mixer)
    h = x
    for i in range(DEPTH):
        mix_norm, mix_params, mlp_norm, w_up, w_down = layers[i]
        h = h + mixers[i % N_MIXERS](rms_norm(h, mix_norm), *mix_params)
        h = h + squared_relu_mlp(rms_norm(h, mlp_norm), w_up, w_down)
    return rms_norm(h, final_norm)
```

```python
import functools
import math

import jax
import jax.numpy as jnp
from jax import lax
from jax.experimental import pallas as pl
from jax.experimental.pallas import tpu as pltpu

F32 = jnp.float32
BF16 = jnp.bfloat16

RMS_EPS = 1e-6
ROPE_THETA = 10000.0
CHUNK = 64
RG_BLOCKS = 8
RG_CONV = 4
RG_C = 8.0
ATT_HEADS = 16
ATT_KV_HEADS = 4
ATT_HEAD_DIM = 128
ATT_GROUP = ATT_HEADS // ATT_KV_HEADS
IDX_HEADS = 16
IDX_DIM = 64
TOPK_MAX = 256
SSD_HEAD_DIM = 64
SSD_GROUPS = 8
SSD_STATE = 128
SSD_CONV = 4

V7X_VMEM_BYTES = 64 * 1024 * 1024
VMEM_LIMIT_BYTES = V7X_VMEM_BYTES - 8 * 1024 * 1024
LANES = 128
SUBLANES = 8

NEG_INF = float("-inf")
KEY_NEG_INF = -2139095041
NEG_INF_BITS = -8388608


def _cparams(n_axes):
    return pltpu.CompilerParams(
        dimension_semantics=("arbitrary",) * n_axes,
        vmem_limit_bytes=VMEM_LIMIT_BYTES,
    )


def _dot(a, b):
    return jnp.dot(a, b, preferred_element_type=F32)


def _dot_nt(a, b):
    return lax.dot_general(a, b, (((1,), (1,)), ((), ())), preferred_element_type=F32)


def _dot_tn(a, b):
    return lax.dot_general(a, b, (((0,), (0,)), ((), ())), preferred_element_type=F32)


def _softplus(x):
    return jnp.maximum(x, 0.0) + jnp.log1p(jnp.exp(-jnp.abs(x)))


def _sigmoid(x):
    return jax.nn.sigmoid(x)


def _mm_kernel(*refs, has_norm, act, has_res, nk):
    it = iter(refs)
    x_ref = next(it)
    g_ref = next(it) if has_norm else None
    w_ref = next(it)
    r_ref = next(it) if has_res else None
    o_ref = next(it)
    acc_ref = next(it) if nk > 1 else None
    xn_ref = next(it) if has_norm else None
    j = pl.program_id(1)
    k = pl.program_id(2)

    if has_norm:
        @pl.when(j == 0)
        def _():
            xf = x_ref[...].astype(F32)
            ms = jnp.mean(xf * xf, axis=-1, keepdims=True)
            xn_ref[...] = (xf * lax.rsqrt(ms + RMS_EPS) * g_ref[...]).astype(BF16)

        lhs = xn_ref[...]
    else:
        lhs = x_ref[...].astype(BF16)

    p = _dot(lhs, w_ref[...])

    def finish(v):
        if act == "relu2":
            v = jnp.maximum(v, 0.0)
            v = v * v
        if has_res:
            v = v + r_ref[...]
        o_ref[...] = v.astype(o_ref.dtype)

    if nk == 1:
        finish(p)
    else:
        @pl.when(k == 0)
        def _():
            acc_ref[...] = p

        @pl.when(jnp.logical_and(k > 0, k < nk - 1))
        def _():
            acc_ref[...] += p

        @pl.when(k == nk - 1)
        def _():
            finish(acc_ref[...] + p)


def _matmul(x, w, *, norm_g=None, act=None, residual=None, out_dtype=F32,
            tm=1024, tn=512, tk=None):
    m, kdim = x.shape
    n = w.shape[1]
    tm = min(tm, m)
    tn = min(tn, n)
    tk = kdim if tk is None else min(tk, kdim)
    has_norm = norm_g is not None
    if has_norm:
        assert tk == kdim
    assert m % tm == 0 and n % tn == 0 and kdim % tk == 0
    nk = kdim // tk
    grid = (m // tm, n // tn, nk)

    in_specs = [pl.BlockSpec((tm, tk), lambda i, j, k: (i, k))]
    args = [x]
    if has_norm:
        in_specs.append(pl.BlockSpec((1, tk), lambda i, j, k: (0, 0)))
        args.append(norm_g.reshape(1, kdim).astype(F32))
    in_specs.append(pl.BlockSpec((tk, tn), lambda i, j, k: (k, j)))
    args.append(w)
    if residual is not None:
        in_specs.append(pl.BlockSpec((tm, tn), lambda i, j, k: (i, j)))
        args.append(residual)
    scratch = []
    if nk > 1:
        scratch.append(pltpu.VMEM((tm, tn), F32))
    if has_norm:
        scratch.append(pltpu.VMEM((tm, tk), BF16))

    return pl.pallas_call(
        functools.partial(_mm_kernel, has_norm=has_norm, act=act,
                          has_res=residual is not None, nk=nk),
        grid=grid,
        in_specs=in_specs,
        out_specs=pl.BlockSpec((tm, tn), lambda i, j, k: (i, j)),
        out_shape=jax.ShapeDtypeStruct((m, n), out_dtype),
        scratch_shapes=scratch,
        compiler_params=_cparams(3),
    )(*args)


def _rmsnorm_kernel(x_ref, g_ref, o_ref):
    xf = x_ref[...]
    ms = jnp.mean(xf * xf, axis=-1, keepdims=True)
    o_ref[...] = xf * lax.rsqrt(ms + RMS_EPS) * g_ref[...]


def _rmsnorm(x, g, tm=512):
    m, d = x.shape
    tm = min(tm, m)
    return pl.pallas_call(
        _rmsnorm_kernel,
        grid=(m // tm,),
        in_specs=[pl.BlockSpec((tm, d), lambda i: (i, 0)),
                  pl.BlockSpec((1, d), lambda i: (0, 0))],
        out_specs=pl.BlockSpec((tm, d), lambda i: (i, 0)),
        out_shape=jax.ShapeDtypeStruct((m, d), F32),
        compiler_params=_cparams(1),
    )(x, g.reshape(1, d))


HALO = SUBLANES


def _causal_conv(buf, x_ref, cw_ref, cb_ref, n_taps):
    t_rows = x_ref.shape[0]
    buf[HALO:HALO + t_rows, :] = x_ref[...]
    out = cb_ref[...]
    for k in range(n_taps):
        off = HALO - (n_taps - 1) + k
        out = out + cw_ref[k:k + 1, :] * buf[off:off + t_rows, :]
    buf[0:HALO, :] = buf[t_rows:t_rows + HALO, :]
    return out


def _rg_kernel(gate_ref, x_ref, cw_ref, cb_ref, wa_ref, wx_ref, ba_ref, bx_ref,
               lam_ref, y_ref, xbuf, a_scr, b_scr, hcar):
    t = pl.program_id(0)
    t_rows, width = x_ref.shape
    n_blocks, bw, _ = wa_ref.shape

    @pl.when(t == 0)
    def _():
        xbuf[0:HALO, :] = jnp.zeros((HALO, width), F32)
        hcar[...] = jnp.zeros_like(hcar)

    xc = _causal_conv(xbuf, x_ref, cw_ref, cb_ref, RG_CONV)
    xcb = xc.astype(BF16)
    sp = _softplus(-lam_ref[...])
    for c in range(n_blocks):
        sl = slice(c * bw, (c + 1) * bw)
        xs = xcb[:, sl]
        r = _sigmoid(_dot(xs, wa_ref[c]) + ba_ref[:, sl])
        i = _sigmoid(_dot(xs, wx_ref[c]) + bx_ref[:, sl])
        log_a = -RG_C * r * sp[:, sl]
        th = jnp.tanh(log_a)
        mult = jnp.sqrt(-2.0 * th / (1.0 - th))
        a_scr[:, sl] = jnp.exp(log_a)
        b_scr[:, sl] = mult * (i * xc[:, sl])

    def step(r, h):
        h = a_scr[pl.ds(r, 1), :] * h + b_scr[pl.ds(r, 1), :]
        b_scr[pl.ds(r, 1), :] = h
        return h

    hcar[...] = lax.fori_loop(0, t_rows, step, hcar[...], unroll=8)
    y_ref[...] = (b_scr[...] * jax.nn.gelu(gate_ref[...])).astype(y_ref.dtype)


def _rg_core(xg, conv_w, conv_b, wa, ba, wx, bx, lam, t_rows=256):
    s, two_w = xg.shape
    width = two_w // 2
    t_rows = min(t_rows, s)
    row = lambda v: v.reshape(1, width).astype(F32)
    full2 = lambda shape: pl.BlockSpec(shape, lambda t: (0, 0))
    full3 = lambda shape: pl.BlockSpec(shape, lambda t: (0, 0, 0))
    return pl.pallas_call(
        _rg_kernel,
        grid=(s // t_rows,),
        in_specs=[
            pl.BlockSpec((t_rows, width), lambda t: (t, 0)),
            pl.BlockSpec((t_rows, width), lambda t: (t, 1)),
            full2((RG_CONV, width)), full2((1, width)),
            full3(wa.shape), full3(wx.shape),
            full2((1, width)), full2((1, width)), full2((1, width)),
        ],
        out_specs=pl.BlockSpec((t_rows, width), lambda t: (t, 0)),
        out_shape=jax.ShapeDtypeStruct((s, width), BF16),
        scratch_shapes=[
            pltpu.VMEM((t_rows + HALO, width), F32),
            pltpu.VMEM((t_rows, width), F32),
            pltpu.VMEM((t_rows, width), F32),
            pltpu.VMEM((1, width), F32),
        ],
        compiler_params=_cparams(1),
    )(xg, xg, conv_w.astype(F32), row(conv_b), wa.astype(BF16), wx.astype(BF16),
      row(ba), row(bx), row(lam))


def _rglru_layer(h, mix_norm, w_in, conv_w, conv_b, w_a, b_a, w_x, b_x, lam, w_out):
    xg = _matmul(h, w_in.astype(BF16), norm_g=mix_norm, out_dtype=F32)
    y = _rg_core(xg, conv_w, conv_b, w_a, b_a, w_x, b_x, lam)
    return _matmul(y, w_out.astype(BF16), residual=h, out_dtype=F32)


SSD_TILE = 128
SSD_HPG = 8
SSD_GW = SSD_HPG * SSD_HEAD_DIM


def _cumsum(v, axis):
    n = v.shape[axis]
    idx = lax.broadcasted_iota(jnp.int32, v.shape, axis)
    shift = 1
    while shift < n:
        v = v + jnp.where(idx >= shift, pltpu.roll(v, shift, axis), 0.0)
        shift *= 2
    return v


def _expand_heads(cols):
    rows = cols.shape[0]
    lane = lax.broadcasted_iota(jnp.int32, (rows, LANES), 1)
    parts = []
    for v in range(SSD_GW // LANES):
        lo = jnp.broadcast_to(cols[:, 2 * v:2 * v + 1], (rows, LANES))
        hi = jnp.broadcast_to(cols[:, 2 * v + 1:2 * v + 2], (rows, LANES))
        parts.append(jnp.where(lane < SSD_HEAD_DIM, lo, hi))
    return jnp.concatenate(parts, axis=1)


def _ssd_kernel(z_ref, x_ref, b_ref, c_ref, cwx_ref, cwb_ref, cwc_ref, cbx_ref,
                cbb_ref, cbc_ref, dtc_ref, dtr_ref, dbc_ref, dbr_ref, alc_ref,
                alr_ref, dsk_ref, ng_ref, y_ref, xbuf, bbuf, cbuf, st):
    t = pl.program_id(1)
    rows = x_ref.shape[0]

    @pl.when(t == 0)
    def _():
        xbuf[0:HALO, :] = jnp.zeros((HALO, xbuf.shape[1]), F32)
        bbuf[0:HALO, :] = jnp.zeros((HALO, bbuf.shape[1]), F32)
        cbuf[0:HALO, :] = jnp.zeros((HALO, cbuf.shape[1]), F32)
        st[...] = jnp.zeros_like(st)

    def conv_silu(buf, ref, cw, cb):
        v = _causal_conv(buf, ref, cw, cb, SSD_CONV)
        return v * _sigmoid(v)

    xs = conv_silu(xbuf, x_ref, cwx_ref, cbx_ref)
    bm = conv_silu(bbuf, b_ref, cwb_ref, cbb_ref)
    cm = conv_silu(cbuf, c_ref, cwc_ref, cbc_ref)
    bmb = bm.astype(BF16)
    cmb = cm.astype(BF16)

    dtc = _softplus(dtc_ref[...] + dbc_ref[...])
    dtr = _softplus(dtr_ref[...] + dbr_ref[...])
    csc = _cumsum(dtc * (-jnp.exp(alc_ref[...])), 0)
    csr = _cumsum(dtr * (-jnp.exp(alr_ref[...])), 1)
    last = csc[rows - 1:rows, :]

    cb = _dot_nt(cmb, bmb)
    li = lax.broadcasted_iota(jnp.int32, (rows, rows), 0)
    si = lax.broadcasted_iota(jnp.int32, (rows, rows), 1)
    causal = li >= si
    lane = lax.broadcasted_iota(jnp.int32, (rows, LANES), 1)

    xdt = xs * _expand_heads(dtc)
    xdtb = xdt.astype(BF16)
    ys = []
    for v in range(SSD_GW // LANES):
        xv = xdtb[:, v * LANES:(v + 1) * LANES]
        pair = []
        for j in (2 * v, 2 * v + 1):
            seg = csc[:, j:j + 1] - csr[j:j + 1, :]
            dec = jnp.exp(jnp.where(causal, seg, NEG_INF))
            pair.append(_dot((cb * dec).astype(BF16), xv))
        ys.append(jnp.where(lane < SSD_HEAD_DIM, pair[0], pair[1]))
    y = jnp.concatenate(ys, axis=1)
    y = y + _dot(cmb, st[...].astype(BF16)) * _expand_heads(jnp.exp(csc))
    y = y + xs * dsk_ref[...]

    z = z_ref[...]
    yz = y * (z * _sigmoid(z))
    yz = yz * lax.rsqrt(jnp.mean(yz * yz, axis=-1, keepdims=True) + RMS_EPS)
    y_ref[...] = (yz * ng_ref[...]).astype(y_ref.dtype)

    xw = (xdt * _expand_heads(jnp.exp(last - csc))).astype(BF16)
    st[...] = st[...] * _expand_heads(jnp.exp(last)) + _dot_tn(bmb, xw)


def _ssd_core(zx, dt_raw, conv_w, conv_b, dt_bias, a_log, d_skip, norm_g):
    s = zx.shape[0]
    g = SSD_GROUPS
    inner = g * SSD_GW
    rows = min(SSD_TILE, s)
    nx = inner // SSD_GW
    nb = 2 * inner // LANES
    nc = nb + g * SSD_STATE // LANES
    cwb0 = inner // LANES
    cwc0 = cwb0 + g * SSD_STATE // LANES

    dth = dt_raw[:, :g * SSD_HPG].reshape(s, g, SSD_HPG)
    dtc = jnp.transpose(dth, (1, 0, 2))
    dtr = jnp.transpose(dth, (1, 2, 0))
    per_head = lambda v: v.astype(F32).reshape(g, SSD_HPG)
    col = lambda v: per_head(v)[:, None, :]
    rowv = lambda v: per_head(v)[:, :, None]
    conv_w = conv_w.astype(F32)
    conv_b = conv_b.reshape(1, -1).astype(F32)
    dsk = jnp.repeat(d_skip.astype(F32), SSD_HEAD_DIM).reshape(1, inner)
    ng = norm_g.reshape(1, inner).astype(F32)

    gs = lambda shape, fn: pl.BlockSpec(shape, fn)
    in_specs = [
        gs((rows, SSD_GW), lambda gi, t: (t, gi)),
        gs((rows, SSD_GW), lambda gi, t: (t, nx + gi)),
        gs((rows, SSD_STATE), lambda gi, t: (t, nb + gi)),
        gs((rows, SSD_STATE), lambda gi, t: (t, nc + gi)),
        gs((SSD_CONV, SSD_GW), lambda gi, t: (0, gi)),
        gs((SSD_CONV, SSD_STATE), lambda gi, t: (0, cwb0 + gi)),
        gs((SSD_CONV, SSD_STATE), lambda gi, t: (0, cwc0 + gi)),
        gs((1, SSD_GW), lambda gi, t: (0, gi)),
        gs((1, SSD_STATE), lambda gi, t: (0, cwb0 + gi)),
        gs((1, SSD_STATE), lambda gi, t: (0, cwc0 + gi)),
        gs((None, rows, SSD_HPG), lambda gi, t: (gi, t, 0)),
        gs((None, SSD_HPG, rows), lambda gi, t: (gi, 0, t)),
        gs((None, 1, SSD_HPG), lambda gi, t: (gi, 0, 0)),
        gs((None, SSD_HPG, 1), lambda gi, t: (gi, 0, 0)),
        gs((None, 1, SSD_HPG), lambda gi, t: (gi, 0, 0)),
        gs((None, SSD_HPG, 1), lambda gi, t: (gi, 0, 0)),
        gs((1, SSD_GW), lambda gi, t: (0, gi)),
        gs((1, SSD_GW), lambda gi, t: (0, gi)),
    ]
    return pl.pallas_call(
        _ssd_kernel,
        grid=(g, s // rows),
        in_specs=in_specs,
        out_specs=pl.BlockSpec((rows, SSD_GW), lambda gi, t: (t, gi)),
        out_shape=jax.ShapeDtypeStruct((s, inner), BF16),
        scratch_shapes=[
            pltpu.VMEM((rows + HALO, SSD_GW), F32),
            pltpu.VMEM((rows + HALO, SSD_STATE), F32),
            pltpu.VMEM((rows + HALO, SSD_STATE), F32),
            pltpu.VMEM((SSD_STATE, SSD_GW), F32),
        ],
        compiler_params=_cparams(2),
    )(zx, zx, zx, zx, conv_w, conv_w, conv_w, conv_b, conv_b, conv_b,
      dtc, dtr, col(dt_bias), rowv(dt_bias), col(a_log), rowv(a_log), dsk, ng)


def _ssd_layer(h, mix_norm, w_in, conv_w, conv_b, dt_bias, a_log, d_skip, norm_g, w_out):
    heads = dt_bias.shape[0]
    n_zx = w_in.shape[1] - heads
    w_zx = w_in[:, :n_zx].astype(BF16)
    w_dt = jnp.pad(w_in[:, n_zx:], ((0, 0), (0, LANES - heads))).astype(BF16)
    zx = _matmul(h, w_zx, norm_g=mix_norm, out_dtype=F32)
    dt_raw = _matmul(h, w_dt, norm_g=mix_norm, out_dtype=F32)
    y = _ssd_core(zx, dt_raw, conv_w, conv_b, dt_bias, a_log, d_skip, norm_g)
    return _matmul(y, w_out.astype(BF16), residual=h, out_dtype=F32)


def _rope_kernel(qkv_ref, kw_ref, cos_ref, sin_ref, cosi_ref, sini_ref,
                 q_ref, k_ref, v_ref, qi_ref, ki_ref, w_ref, *, q_scale, w_scale):
    rows = qkv_ref.shape[0]
    lane = lax.broadcasted_iota(jnp.int32, (rows, LANES), 1)
    cos, sin = cos_ref[...], sin_ref[...]
    cosi, sini = cosi_ref[...], sini_ref[...]
    half_i = IDX_DIM // 2
    lower_i = (lane & (IDX_DIM - 1)) < half_i

    def rope_head(x):
        return x * cos + pltpu.roll(x, ATT_HEAD_DIM // 2, 1) * sin

    def rope_idx(x):
        rot = jnp.where(lower_i, pltpu.roll(x, LANES - half_i, 1), pltpu.roll(x, half_i, 1))
        return x * cosi + rot * sini

    nq = q_ref.shape[1] // LANES
    nkv = k_ref.shape[1] // LANES
    nqi = qi_ref.shape[1] // LANES
    for hh in range(nq):
        x = qkv_ref[:, hh * LANES:(hh + 1) * LANES]
        q_ref[:, hh * LANES:(hh + 1) * LANES] = (rope_head(x) * q_scale).astype(q_ref.dtype)
    off = nq
    for hh in range(nkv):
        x = qkv_ref[:, (off + hh) * LANES:(off + hh + 1) * LANES]
        k_ref[:, hh * LANES:(hh + 1) * LANES] = rope_head(x).astype(k_ref.dtype)
    off += nkv
    v_ref[...] = qkv_ref[:, off * LANES:(off + nkv) * LANES].astype(v_ref.dtype)
    off += nkv
    for hh in range(nqi):
        x = qkv_ref[:, (off + hh) * LANES:(off + hh + 1) * LANES]
        qi_ref[:, hh * LANES:(hh + 1) * LANES] = rope_idx(x).astype(qi_ref.dtype)
    kw = kw_ref[...]
    swapped = pltpu.roll(kw, LANES // 2, 1)
    ki_ref[...] = rope_idx(jnp.where(lane < IDX_DIM, kw, swapped)).astype(ki_ref.dtype)
    w_ref[...] = jnp.where(lane < IDX_HEADS, swapped * w_scale, 0.0)


def _rope_tables(seq):
    def table(dim):
        inv = ROPE_THETA ** (-jnp.arange(0, dim, 2, dtype=F32) / dim)
        ang = jnp.arange(seq, dtype=F32)[:, None] * inv[None, :]
        return jnp.cos(ang), jnp.sin(ang)

    c, s = table(ATT_HEAD_DIM)
    ci, si = table(IDX_DIM)
    return (jnp.concatenate([c, c], axis=1), jnp.concatenate([-s, s], axis=1),
            jnp.concatenate([ci, ci, ci, ci], axis=1), jnp.concatenate([-si, si, -si, si], axis=1))


def _dsa_rope(qkv, kw, t_rows=256):
    s = qkv.shape[0]
    t_rows = min(t_rows, s)
    dq = ATT_HEADS * ATT_HEAD_DIM
    dkv = ATT_KV_HEADS * ATT_HEAD_DIM
    dqi = IDX_HEADS * IDX_DIM
    cos, sin, cosi, sini = _rope_tables(s)
    tile = lambda width: pl.BlockSpec((t_rows, width), lambda t: (t, 0))
    return pl.pallas_call(
        functools.partial(_rope_kernel, q_scale=ATT_HEAD_DIM ** -0.5,
                          w_scale=IDX_HEADS ** -0.5 * IDX_DIM ** -0.5),
        grid=(s // t_rows,),
        in_specs=[tile(qkv.shape[1]), tile(LANES), tile(LANES), tile(LANES), tile(LANES), tile(LANES)],
        out_specs=[tile(dq), tile(dkv), tile(dkv), tile(dqi), tile(LANES), tile(LANES)],
        out_shape=[
            jax.ShapeDtypeStruct((s, dq), BF16),
            jax.ShapeDtypeStruct((s, dkv), BF16),
            jax.ShapeDtypeStruct((s, dkv), BF16),
            jax.ShapeDtypeStruct((s, dqi), BF16),
            jax.ShapeDtypeStruct((s, LANES), BF16),
            jax.ShapeDtypeStruct((s, LANES), F32),
        ],
        compiler_params=_cparams(1),
    )(qkv, kw, cos, sin, cosi, sini)


DSA_TQ = 128
DSA_KB = 512


def _sortable_key(score):
    bits = lax.bitcast_convert_type(score, jnp.int32)
    return jnp.where(bits < 0, bits ^ jnp.int32(0x7FFFFFFF), bits)


def _tile_lanes(v, width):
    return jnp.concatenate([v] * (width // v.shape[1]), axis=1)


def _dsa_kernel(q_ref, qi_ref, w_ref, k_ref, v_ref, ki_ref, o_ref,
                sc, qa, wb, qs, m_scr, l_scr, acc_scr, *, top_k):
    i = pl.program_id(0)
    tq = q_ref.shape[0]
    kb_size = sc.shape[2]
    nkb = ((i + 1) * tq + kb_size - 1) // kb_size
    lane = lax.broadcasted_iota(jnp.int32, (tq, LANES), 1)
    chunk_shift = CHUNK.bit_length() - 1
    q_chunk = (i * tq + lax.broadcasted_iota(jnp.int32, (tq, kb_size), 0)) >> chunk_shift
    k_lane = lax.broadcasted_iota(jnp.int32, (tq, kb_size), 1)

    def admissible(kb):
        return ((kb * kb_size + k_lane) >> chunk_shift) <= q_chunk

    for h in range(IDX_HEADS):
        pair = qi_ref[:, (h // 2) * LANES:(h // 2 + 1) * LANES]
        keep = (lane < IDX_DIM) if h % 2 == 0 else (lane >= IDX_DIM)
        qa[h] = jnp.where(keep, pair, jnp.zeros_like(pair))
        wb[h] = jnp.broadcast_to(w_ref[:, h:h + 1], (tq, LANES))

    def score_block(kb, carry):
        kis = ki_ref[pl.ds(pl.multiple_of(kb * kb_size, kb_size), kb_size), :]
        score = jnp.zeros((tq, kb_size), F32)
        for h in range(IDX_HEADS):
            lg = _dot_nt(qa[h], kis)
            score = score + jnp.maximum(lg, 0.0) * _tile_lanes(wb[h], kb_size)
        sc[kb] = jnp.where(admissible(kb), _sortable_key(score), jnp.int32(KEY_NEG_INF))
        return carry

    lax.fori_loop(0, nkb, score_block, 0)

    def bit_step(b, thr):
        cand = thr + jnp.left_shift(jnp.int32(1), 31 - b)
        cand_t = _tile_lanes(cand, kb_size)

        def count_block(kb, acc):
            ge = jnp.where(sc[kb] >= cand_t, 1.0, 0.0)
            part = ge[:, 0:LANES]
            for c in range(1, kb_size // LANES):
                part = part + ge[:, c * LANES:(c + 1) * LANES]
            return acc + part

        acc = lax.fori_loop(0, nkb, count_block, jnp.zeros((tq, LANES), F32))
        cnt = jnp.broadcast_to(jnp.sum(acc, axis=1, keepdims=True), (tq, LANES))
        return jnp.where(cnt >= float(top_k), cand, thr)

    thr = lax.fori_loop(0, 32, bit_step,
                        jnp.full((tq, LANES), jnp.iinfo(jnp.int32).min, jnp.int32))
    thr_t = _tile_lanes(thr, kb_size)

    def bias_block(kb, carry):
        sel = jnp.logical_and(sc[kb] >= thr_t, admissible(kb))
        sc[kb] = jnp.where(sel, jnp.int32(0), jnp.int32(NEG_INF_BITS))
        return carry

    lax.fori_loop(0, nkb, bias_block, 0)

    for g in range(ATT_KV_HEADS):
        qs[g] = jnp.concatenate(
            [q_ref[:, (g * ATT_GROUP + a) * LANES:(g * ATT_GROUP + a + 1) * LANES]
             for a in range(ATT_GROUP)], axis=0)
    m_scr[...] = jnp.full(m_scr.shape, NEG_INF, F32)
    l_scr[...] = jnp.zeros_like(l_scr)
    acc_scr[...] = jnp.zeros_like(acc_scr)

    def attn_block(kb, carry):
        start = pl.multiple_of(kb * kb_size, kb_size)
        bias = lax.bitcast_convert_type(sc[kb], F32)
        bias = jnp.concatenate([bias] * ATT_GROUP, axis=0)
        for g in range(ATT_KV_HEADS):
            kblk = k_ref[pl.ds(start, kb_size), g * LANES:(g + 1) * LANES]
            vblk = v_ref[pl.ds(start, kb_size), g * LANES:(g + 1) * LANES]
            s = _dot_nt(qs[g], kblk) + bias
            m_old = m_scr[g]
            m_new = jnp.maximum(m_old, jnp.max(s, axis=1, keepdims=True))
            m_safe = jnp.where(m_new == NEG_INF, 0.0, m_new)
            alpha = jnp.exp(m_old - m_safe)
            p = jnp.exp(s - _tile_lanes(m_safe, kb_size))
            l_scr[g] = alpha * l_scr[g] + jnp.sum(p, axis=1, keepdims=True)
            acc_scr[g] = alpha * acc_scr[g] + _dot(p.astype(BF16), vblk)
            m_scr[g] = m_new
        return carry

    lax.fori_loop(0, nkb, attn_block, 0)

    for g in range(ATT_KV_HEADS):
        out = acc_scr[g] / l_scr[g]
        for a in range(ATT_GROUP):
            hh = g * ATT_GROUP + a
            o_ref[:, hh * LANES:(hh + 1) * LANES] = out[a * tq:(a + 1) * tq].astype(o_ref.dtype)


def _dsa_attend(q, k, v, qi, ki, w):
    s = q.shape[0]
    tq = min(DSA_TQ, s)
    kb = min(DSA_KB, s)
    top_k = min(TOPK_MAX, s // 4)
    resident = lambda width: pl.BlockSpec((s, width), lambda i: (0, 0),
                                          pipeline_mode=pl.Buffered(1))
    tile = lambda width: pl.BlockSpec((tq, width), lambda i: (i, 0))
    return pl.pallas_call(
        functools.partial(_dsa_kernel, top_k=top_k),
        grid=(s // tq,),
        in_specs=[tile(q.shape[1]), tile(qi.shape[1]), tile(LANES),
                  resident(k.shape[1]), resident(v.shape[1]), resident(LANES)],
        out_specs=tile(q.shape[1]),
        out_shape=jax.ShapeDtypeStruct(q.shape, BF16),
        scratch_shapes=[
            pltpu.VMEM((s // kb, tq, kb), jnp.int32),
            pltpu.VMEM((IDX_HEADS, tq, LANES), BF16),
            pltpu.VMEM((IDX_HEADS, tq, LANES), F32),
            pltpu.VMEM((ATT_KV_HEADS, ATT_GROUP * tq, LANES), BF16),
            pltpu.VMEM((ATT_KV_HEADS, ATT_GROUP * tq, LANES), F32),
            pltpu.VMEM((ATT_KV_HEADS, ATT_GROUP * tq, LANES), F32),
            pltpu.VMEM((ATT_KV_HEADS, ATT_GROUP * tq, LANES), F32),
        ],
        compiler_params=_cparams(1),
    )(q, qi, w, k, v, ki)


def _dsa_layer(h, mix_norm, w_in, w_out):
    n_main = (ATT_HEADS + 2 * ATT_KV_HEADS) * ATT_HEAD_DIM + IDX_HEADS * IDX_DIM
    n_rest = w_in.shape[1] - n_main
    w_main = w_in[:, :n_main].astype(BF16)
    w_rest = jnp.pad(w_in[:, n_main:], ((0, 0), (0, LANES - n_rest))).astype(BF16)
    qkv = _matmul(h, w_main, norm_g=mix_norm, out_dtype=F32)
    kw = _matmul(h, w_rest, norm_g=mix_norm, out_dtype=F32)
    q, k, v, qi, ki, w = _dsa_rope(qkv, kw)
    o = _dsa_attend(q, k, v, qi, ki, w)
    return _matmul(o, w_out.astype(BF16), residual=h, out_dtype=F32)


def _mlp(h, norm_g, w_up, w_down):
    u = _matmul(h, w_up.astype(BF16), norm_g=norm_g, act="relu2", out_dtype=BF16)
    return _matmul(u, w_down.astype(BF16), residual=h, out_dtype=F32,
                   tm=1024, tn=1024, tk=2048)


def kernel(x, l0_mix_norm, l0_rg_in_w, l0_rg_conv_w, l0_rg_conv_b, l0_rg_wa, l0_rg_ba, l0_rg_wx, l0_rg_bx, l0_rg_lambda, l0_rg_out_w, l0_mlp_norm, l0_mlp_up, l0_mlp_down, l1_mix_norm, l1_dsa_in_w, l1_dsa_out_w, l1_mlp_norm, l1_mlp_up, l1_mlp_down, l2_mix_norm, l2_ssd_in_w, l2_ssd_conv_w, l2_ssd_conv_b, l2_ssd_dt_bias, l2_ssd_a_log, l2_ssd_d, l2_ssd_norm, l2_ssd_out_w, l2_mlp_norm, l2_mlp_up, l2_mlp_down, l3_mix_norm, l3_rg_in_w, l3_rg_conv_w, l3_rg_conv_b, l3_rg_wa, l3_rg_ba, l3_rg_wx, l3_rg_bx, l3_rg_lambda, l3_rg_out_w, l3_mlp_norm, l3_mlp_up, l3_mlp_down, final_norm):
    bsz, seq, d = x.shape
    outs = []
    for b in range(bsz):
        h = x[b]
        h = _rglru_layer(h, l0_mix_norm, l0_rg_in_w, l0_rg_conv_w, l0_rg_conv_b, l0_rg_wa,
                         l0_rg_ba, l0_rg_wx, l0_rg_bx, l0_rg_lambda, l0_rg_out_w)
        h = _mlp(h, l0_mlp_norm, l0_mlp_up, l0_mlp_down)
        h = _dsa_layer(h, l1_mix_norm, l1_dsa_in_w, l1_dsa_out_w)
        h = _mlp(h, l1_mlp_norm, l1_mlp_up, l1_mlp_down)
        h = _ssd_layer(h, l2_mix_norm, l2_ssd_in_w, l2_ssd_conv_w, l2_ssd_conv_b, l2_ssd_dt_bias,
                       l2_ssd_a_log, l2_ssd_d, l2_ssd_norm, l2_ssd_out_w)
        h = _mlp(h, l2_mlp_norm, l2_mlp_up, l2_mlp_down)
        h = _rglru_layer(h, l3_mix_norm, l3_rg_in_w, l3_rg_conv_w, l3_rg_conv_b, l3_rg_wa,
                         l3_rg_ba, l3_rg_wx, l3_rg_bx, l3_rg_lambda, l3_rg_out_w)
        h = _mlp(h, l3_mlp_norm, l3_mlp_up, l3_mlp_down)
        outs.append(_rmsnorm(h, final_norm))
    return jnp.stack(outs, axis=0)
```

```python
import functools
import math

import jax
import jax.numpy as jnp
from jax import lax
from jax.experimental import pallas as pl
from jax.experimental.pallas import tpu as pltpu

F32 = jnp.float32
BF16 = jnp.bfloat16

RMS_EPS = 1e-6
ROPE_THETA = 10000.0
CHUNK = 64
RG_BLOCKS = 8
RG_CONV = 4
RG_C = 8.0
ATT_HEADS = 16
ATT_KV_HEADS = 4
ATT_HEAD_DIM = 128
ATT_GROUP = ATT_HEADS // ATT_KV_HEADS
IDX_HEADS = 16
IDX_DIM = 64
TOPK_MAX = 256
SSD_HEAD_DIM = 64
SSD_GROUPS = 8
SSD_STATE = 128
SSD_CONV = 4

V7X_VMEM_BYTES = 64 * 1024 * 1024
VMEM_LIMIT_BYTES = V7X_VMEM_BYTES - 8 * 1024 * 1024
LANES = 128
SUBLANES = 8

NEG_INF = float("-inf")
KEY_NEG_INF = -2139095041
NEG_INF_BITS = -8388608


def _cparams(n_axes):
    return pltpu.CompilerParams(
        dimension_semantics=("arbitrary",) * n_axes,
        vmem_limit_bytes=VMEM_LIMIT_BYTES,
    )


def _dot(a, b):
    return jnp.dot(a, b, preferred_element_type=F32)


def _dot_nt(a, b):
    return lax.dot_general(a, b, (((1,), (1,)), ((), ())), preferred_element_type=F32)


def _dot_tn(a, b):
    return lax.dot_general(a, b, (((0,), (0,)), ((), ())), preferred_element_type=F32)


def _softplus(x):
    return jnp.maximum(x, 0.0) + jnp.log1p(jnp.exp(-jnp.abs(x)))


def _sigmoid(x):
    return jax.nn.sigmoid(x)


def _mm_kernel(*refs, has_norm, act, has_res, nk):
    it = iter(refs)
    x_ref = next(it)
    g_ref = next(it) if has_norm else None
    w_ref = next(it)
    r_ref = next(it) if has_res else None
    o_ref = next(it)
    acc_ref = next(it) if nk > 1 else None
    xn_ref = next(it) if has_norm else None
    j = pl.program_id(1)
    k = pl.program_id(2)

    if has_norm:
        @pl.when(j == 0)
        def _():
            xf = x_ref[...].astype(F32)
            ms = jnp.mean(xf * xf, axis=-1, keepdims=True)
            xn_ref[...] = (xf * lax.rsqrt(ms + RMS_EPS) * g_ref[...]).astype(BF16)

        lhs = xn_ref[...]
    else:
        lhs = x_ref[...].astype(BF16)

    p = _dot(lhs, w_ref[...].astype(BF16))

    def finish(v):
        if act == "relu2":
            v = jnp.maximum(v, 0.0)
            v = v * v
        if has_res:
            v = v + r_ref[...]
        o_ref[...] = v.astype(o_ref.dtype)

    if nk == 1:
        finish(p)
    else:
        @pl.when(k == 0)
        def _():
            acc_ref[...] = p

        @pl.when(jnp.logical_and(k > 0, k < nk - 1))
        def _():
            acc_ref[...] += p

        @pl.when(k == nk - 1)
        def _():
            finish(acc_ref[...] + p)


def _matmul(x, w, *, norm_g=None, act=None, residual=None, out_dtype=F32,
            tm=1024, tn=512, tk=None, n_cols=None, name="matmul"):
    m, kdim = x.shape
    n = w.shape[1] if n_cols is None else n_cols
    tm = min(tm, m)
    tn = min(tn, n)
    tk = kdim if tk is None else min(tk, kdim)
    has_norm = norm_g is not None
    if has_norm:
        assert tk == kdim
    assert m % tm == 0 and n % tn == 0 and kdim % tk == 0
    nk = kdim // tk
    grid = (m // tm, n // tn, nk)

    in_specs = [pl.BlockSpec((tm, tk), lambda i, j, k: (i, k))]
    args = [x]
    if has_norm:
        in_specs.append(pl.BlockSpec((1, tk), lambda i, j, k: (0, 0)))
        args.append(norm_g.reshape(1, kdim).astype(F32))
    in_specs.append(pl.BlockSpec((tk, tn), lambda i, j, k: (k, j)))
    args.append(w)
    if residual is not None:
        in_specs.append(pl.BlockSpec((tm, tn), lambda i, j, k: (i, j)))
        args.append(residual)
    scratch = []
    if nk > 1:
        scratch.append(pltpu.VMEM((tm, tn), F32))
    if has_norm:
        scratch.append(pltpu.VMEM((tm, tk), BF16))

    return pl.pallas_call(
        functools.partial(_mm_kernel, has_norm=has_norm, act=act,
                          has_res=residual is not None, nk=nk),
        grid=grid,
        in_specs=in_specs,
        out_specs=pl.BlockSpec((tm, tn), lambda i, j, k: (i, j)),
        out_shape=jax.ShapeDtypeStruct((m, n), out_dtype),
        scratch_shapes=scratch,
        compiler_params=_cparams(3),
        name=name,
    )(*args)


def _rmsnorm_kernel(x_ref, g_ref, o_ref):
    xf = x_ref[...]
    ms = jnp.mean(xf * xf, axis=-1, keepdims=True)
    o_ref[...] = xf * lax.rsqrt(ms + RMS_EPS) * g_ref[...]


def _rmsnorm(x, g, tm=512):
    m, d = x.shape
    tm = min(tm, m)
    return pl.pallas_call(
        _rmsnorm_kernel,
        grid=(m // tm,),
        in_specs=[pl.BlockSpec((tm, d), lambda i: (i, 0)),
                  pl.BlockSpec((1, d), lambda i: (0, 0))],
        out_specs=pl.BlockSpec((tm, d), lambda i: (i, 0)),
        out_shape=jax.ShapeDtypeStruct((m, d), F32),
        compiler_params=_cparams(1),
        name="final_norm",
    )(x, g.reshape(1, d))


HALO = SUBLANES


def _causal_conv(buf, x_ref, cw_ref, cb_ref, n_taps):
    t_rows = x_ref.shape[0]
    buf[HALO:HALO + t_rows, :] = x_ref[...]
    out = cb_ref[...]
    for k in range(n_taps):
        off = HALO - (n_taps - 1) + k
        out = out + cw_ref[k:k + 1, :] * buf[off:off + t_rows, :]
    buf[0:HALO, :] = buf[t_rows:t_rows + HALO, :]
    return out


def _rg_kernel(gate_ref, x_ref, cw_ref, cb_ref, wa_ref, wx_ref, ba_ref, bx_ref,
               lam_ref, y_ref, xbuf, a_scr, b_scr, hcar):
    t = pl.program_id(0)
    t_rows, width = x_ref.shape
    n_blocks, bw, _ = wa_ref.shape

    @pl.when(t == 0)
    def _():
        xbuf[0:HALO, :] = jnp.zeros((HALO, width), F32)
        hcar[...] = jnp.zeros_like(hcar)

    xc = _causal_conv(xbuf, x_ref, cw_ref, cb_ref, RG_CONV)
    xcb = xc.astype(BF16)
    sp = _softplus(-lam_ref[...])
    for c in range(n_blocks):
        sl = slice(c * bw, (c + 1) * bw)
        xs = xcb[:, sl]
        r = _sigmoid(_dot(xs, wa_ref[c]) + ba_ref[:, sl])
        i = _sigmoid(_dot(xs, wx_ref[c]) + bx_ref[:, sl])
        log_a = -RG_C * r * sp[:, sl]
        th = jnp.tanh(log_a)
        mult = jnp.sqrt(-2.0 * th / (1.0 - th))
        a_scr[:, sl] = jnp.exp(log_a)
        b_scr[:, sl] = mult * (i * xc[:, sl])

    def step(r, h):
        h = a_scr[pl.ds(r, 1), :] * h + b_scr[pl.ds(r, 1), :]
        b_scr[pl.ds(r, 1), :] = h
        return h

    hcar[...] = lax.fori_loop(0, t_rows, step, hcar[...], unroll=8)
    y_ref[...] = (b_scr[...] * jax.nn.gelu(gate_ref[...])).astype(y_ref.dtype)


def _rg_core(xg, conv_w, conv_b, wa, ba, wx, bx, lam, t_rows=256):
    s, two_w = xg.shape
    width = two_w // 2
    t_rows = min(t_rows, s)
    row = lambda v: v.reshape(1, width).astype(F32)
    full2 = lambda shape: pl.BlockSpec(shape, lambda t: (0, 0))
    full3 = lambda shape: pl.BlockSpec(shape, lambda t: (0, 0, 0))
    return pl.pallas_call(
        _rg_kernel,
        grid=(s // t_rows,),
        in_specs=[
            pl.BlockSpec((t_rows, width), lambda t: (t, 0)),
            pl.BlockSpec((t_rows, width), lambda t: (t, 1)),
            full2((RG_CONV, width)), full2((1, width)),
            full3(wa.shape), full3(wx.shape),
            full2((1, width)), full2((1, width)), full2((1, width)),
        ],
        out_specs=pl.BlockSpec((t_rows, width), lambda t: (t, 0)),
        out_shape=jax.ShapeDtypeStruct((s, width), BF16),
        scratch_shapes=[
            pltpu.VMEM((t_rows + HALO, width), F32),
            pltpu.VMEM((t_rows, width), F32),
            pltpu.VMEM((t_rows, width), F32),
            pltpu.VMEM((1, width), F32),
        ],
        compiler_params=_cparams(1),
        name="rg_core",
    )(xg, xg, conv_w.astype(F32), row(conv_b), wa.astype(BF16), wx.astype(BF16),
      row(ba), row(bx), row(lam))


def _rglru_layer(h, mix_norm, w_in, conv_w, conv_b, w_a, b_a, w_x, b_x, lam, w_out):
    xg = _matmul(h, w_in, norm_g=mix_norm, out_dtype=F32, name="rg_in")
    y = _rg_core(xg, conv_w, conv_b, w_a, b_a, w_x, b_x, lam)
    return _matmul(y, w_out, residual=h, out_dtype=F32, name="rg_out")


SSD_TILE = 128
SSD_HPG = 8
SSD_GW = SSD_HPG * SSD_HEAD_DIM


def _cumsum(v, axis):
    n = v.shape[axis]
    idx = lax.broadcasted_iota(jnp.int32, v.shape, axis)
    shift = 1
    while shift < n:
        v = v + jnp.where(idx >= shift, pltpu.roll(v, shift, axis), 0.0)
        shift *= 2
    return v


def _expand_heads(cols):
    rows = cols.shape[0]
    lane = lax.broadcasted_iota(jnp.int32, (rows, LANES), 1)
    parts = []
    for v in range(SSD_GW // LANES):
        lo = jnp.broadcast_to(cols[:, 2 * v:2 * v + 1], (rows, LANES))
        hi = jnp.broadcast_to(cols[:, 2 * v + 1:2 * v + 2], (rows, LANES))
        parts.append(jnp.where(lane < SSD_HEAD_DIM, lo, hi))
    return jnp.concatenate(parts, axis=1)


def _ssd_kernel(z_ref, x_ref, b_ref, c_ref, cwx_ref, cwb_ref, cwc_ref, cbx_ref,
                cbb_ref, cbc_ref, dtc_ref, dtr_ref, dbc_ref, dbr_ref, alc_ref,
                alr_ref, dsk_ref, ng_ref, y_ref, xbuf, bbuf, cbuf, st):
    t = pl.program_id(1)
    rows = x_ref.shape[0]

    @pl.when(t == 0)
    def _():
        xbuf[0:HALO, :] = jnp.zeros((HALO, xbuf.shape[1]), F32)
        bbuf[0:HALO, :] = jnp.zeros((HALO, bbuf.shape[1]), F32)
        cbuf[0:HALO, :] = jnp.zeros((HALO, cbuf.shape[1]), F32)
        st[...] = jnp.zeros_like(st)

    def conv_silu(buf, ref, cw, cb):
        v = _causal_conv(buf, ref, cw, cb, SSD_CONV)
        return v * _sigmoid(v)

    xs = conv_silu(xbuf, x_ref, cwx_ref, cbx_ref)
    bm = conv_silu(bbuf, b_ref, cwb_ref, cbb_ref)
    cm = conv_silu(cbuf, c_ref, cwc_ref, cbc_ref)
    bmb = bm.astype(BF16)
    cmb = cm.astype(BF16)

    dtc = _softplus(dtc_ref[...] + dbc_ref[...])
    dtr = _softplus(dtr_ref[...] + dbr_ref[...])
    csc = _cumsum(dtc * (-jnp.exp(alc_ref[...])), 0)
    csr = _cumsum(dtr * (-jnp.exp(alr_ref[...])), 1)
    last = csc[rows - 1:rows, :]

    cb = _dot_nt(cmb, bmb)
    li = lax.broadcasted_iota(jnp.int32, (rows, rows), 0)
    si = lax.broadcasted_iota(jnp.int32, (rows, rows), 1)
    causal = li >= si
    lane = lax.broadcasted_iota(jnp.int32, (rows, LANES), 1)

    xdt = xs * _expand_heads(dtc)
    xdtb = xdt.astype(BF16)
    ys = []
    for v in range(SSD_GW // LANES):
        xv = xdtb[:, v * LANES:(v + 1) * LANES]
        pair = []
        for j in (2 * v, 2 * v + 1):
            seg = csc[:, j:j + 1] - csr[j:j + 1, :]
            dec = jnp.exp(jnp.where(causal, seg, NEG_INF))
            pair.append(_dot((cb * dec).astype(BF16), xv))
        ys.append(jnp.where(lane < SSD_HEAD_DIM, pair[0], pair[1]))
    y = jnp.concatenate(ys, axis=1)
    y = y + _dot(cmb, st[...].astype(BF16)) * _expand_heads(jnp.exp(csc))
    y = y + xs * dsk_ref[...]

    z = z_ref[...]
    yz = y * (z * _sigmoid(z))
    yz = yz * lax.rsqrt(jnp.mean(yz * yz, axis=-1, keepdims=True) + RMS_EPS)
    y_ref[...] = (yz * ng_ref[...]).astype(y_ref.dtype)

    xw = (xdt * _expand_heads(jnp.exp(last - csc))).astype(BF16)
    st[...] = st[...] * _expand_heads(jnp.exp(last)) + _dot_tn(bmb, xw)


def _ssd_core(zx, dt_raw, conv_w, conv_b, dt_bias, a_log, d_skip, norm_g):
    s = zx.shape[0]
    g = SSD_GROUPS
    inner = g * SSD_GW
    rows = min(SSD_TILE, s)
    nx = inner // SSD_GW
    nb = 2 * inner // LANES
    nc = nb + g * SSD_STATE // LANES
    cwb0 = inner // LANES
    cwc0 = cwb0 + g * SSD_STATE // LANES

    dth = dt_raw[:, :g * SSD_HPG].reshape(s, g, SSD_HPG)
    dtc = jnp.transpose(dth, (1, 0, 2))
    dtr = jnp.transpose(dth, (1, 2, 0))
    per_head = lambda v: v.astype(F32).reshape(g, SSD_HPG)
    col = lambda v: per_head(v)[:, None, :]
    rowv = lambda v: per_head(v)[:, :, None]
    conv_w = conv_w.astype(F32)
    conv_b = conv_b.reshape(1, -1).astype(F32)
    dsk = jnp.repeat(d_skip.astype(F32), SSD_HEAD_DIM).reshape(1, inner)
    ng = norm_g.reshape(1, inner).astype(F32)

    gs = lambda shape, fn: pl.BlockSpec(shape, fn)
    in_specs = [
        gs((rows, SSD_GW), lambda gi, t: (t, gi)),
        gs((rows, SSD_GW), lambda gi, t: (t, nx + gi)),
        gs((rows, SSD_STATE), lambda gi, t: (t, nb + gi)),
        gs((rows, SSD_STATE), lambda gi, t: (t, nc + gi)),
        gs((SSD_CONV, SSD_GW), lambda gi, t: (0, gi)),
        gs((SSD_CONV, SSD_STATE), lambda gi, t: (0, cwb0 + gi)),
        gs((SSD_CONV, SSD_STATE), lambda gi, t: (0, cwc0 + gi)),
        gs((1, SSD_GW), lambda gi, t: (0, gi)),
        gs((1, SSD_STATE), lambda gi, t: (0, cwb0 + gi)),
        gs((1, SSD_STATE), lambda gi, t: (0, cwc0 + gi)),
        gs((None, rows, SSD_HPG), lambda gi, t: (gi, t, 0)),
        gs((None, SSD_HPG, rows), lambda gi, t: (gi, 0, t)),
        gs((None, 1, SSD_HPG), lambda gi, t: (gi, 0, 0)),
        gs((None, SSD_HPG, 1), lambda gi, t: (gi, 0, 0)),
        gs((None, 1, SSD_HPG), lambda gi, t: (gi, 0, 0)),
        gs((None, SSD_HPG, 1), lambda gi, t: (gi, 0, 0)),
        gs((1, SSD_GW), lambda gi, t: (0, gi)),
        gs((1, SSD_GW), lambda gi, t: (0, gi)),
    ]
    return pl.pallas_call(
        _ssd_kernel,
        grid=(g, s // rows),
        in_specs=in_specs,
        out_specs=pl.BlockSpec((rows, SSD_GW), lambda gi, t: (t, gi)),
        out_shape=jax.ShapeDtypeStruct((s, inner), BF16),
        scratch_shapes=[
            pltpu.VMEM((rows + HALO, SSD_GW), F32),
            pltpu.VMEM((rows + HALO, SSD_STATE), F32),
            pltpu.VMEM((rows + HALO, SSD_STATE), F32),
            pltpu.VMEM((SSD_STATE, SSD_GW), F32),
        ],
        compiler_params=_cparams(2),
        name="ssd_core",
    )(zx, zx, zx, zx, conv_w, conv_w, conv_w, conv_b, conv_b, conv_b,
      dtc, dtr, col(dt_bias), rowv(dt_bias), col(a_log), rowv(a_log), dsk, ng)


def _ssd_layer(h, mix_norm, w_in, conv_w, conv_b, dt_bias, a_log, d_skip, norm_g, w_out):
    heads = dt_bias.shape[0]
    n_zx = w_in.shape[1] - heads
    w_dt = jnp.pad(w_in[:, n_zx:], ((0, 0), (0, LANES - heads)))
    zx = _matmul(h, w_in, norm_g=mix_norm, out_dtype=F32, n_cols=n_zx, name="ssd_in")
    dt_raw = _matmul(h, w_dt, norm_g=mix_norm, out_dtype=F32, name="ssd_in_dt")
    y = _ssd_core(zx, dt_raw, conv_w, conv_b, dt_bias, a_log, d_skip, norm_g)
    return _matmul(y, w_out, residual=h, out_dtype=F32, name="ssd_out")


def _rope_kernel(qkv_ref, kw_ref, cos_ref, sin_ref, cosi_ref, sini_ref,
                 q_ref, k_ref, v_ref, qi_ref, ki_ref, w_ref, *, q_scale, w_scale):
    rows = qkv_ref.shape[0]
    lane = lax.broadcasted_iota(jnp.int32, (rows, LANES), 1)
    cos, sin = cos_ref[...], sin_ref[...]
    cosi, sini = cosi_ref[...], sini_ref[...]
    half_i = IDX_DIM // 2
    lower_i = (lane & (IDX_DIM - 1)) < half_i

    def rope_head(x):
        return x * cos + pltpu.roll(x, ATT_HEAD_DIM // 2, 1) * sin

    def rope_idx(x):
        rot = jnp.where(lower_i, pltpu.roll(x, LANES - half_i, 1), pltpu.roll(x, half_i, 1))
        return x * cosi + rot * sini

    nq = q_ref.shape[1] // LANES
    nkv = k_ref.shape[1] // LANES
    nqi = qi_ref.shape[1] // LANES
    for hh in range(nq):
        x = qkv_ref[:, hh * LANES:(hh + 1) * LANES]
        q_ref[:, hh * LANES:(hh + 1) * LANES] = (rope_head(x) * q_scale).astype(q_ref.dtype)
    off = nq
    for hh in range(nkv):
        x = qkv_ref[:, (off + hh) * LANES:(off + hh + 1) * LANES]
        k_ref[:, hh * LANES:(hh + 1) * LANES] = rope_head(x).astype(k_ref.dtype)
    off += nkv
    v_ref[...] = qkv_ref[:, off * LANES:(off + nkv) * LANES].astype(v_ref.dtype)
    off += nkv
    for hh in range(nqi):
        x = qkv_ref[:, (off + hh) * LANES:(off + hh + 1) * LANES]
        qi_ref[:, hh * LANES:(hh + 1) * LANES] = rope_idx(x).astype(qi_ref.dtype)
    kw = kw_ref[...]
    swapped = pltpu.roll(kw, LANES // 2, 1)
    ki_ref[...] = rope_idx(jnp.where(lane < IDX_DIM, kw, swapped)).astype(ki_ref.dtype)
    w_ref[...] = jnp.where(lane < IDX_HEADS, swapped * w_scale, 0.0)


def _rope_tables(seq):
    def table(dim):
        inv = ROPE_THETA ** (-jnp.arange(0, dim, 2, dtype=F32) / dim)
        ang = jnp.arange(seq, dtype=F32)[:, None] * inv[None, :]
        return jnp.cos(ang), jnp.sin(ang)

    c, s = table(ATT_HEAD_DIM)
    ci, si = table(IDX_DIM)
    return (jnp.concatenate([c, c], axis=1), jnp.concatenate([-s, s], axis=1),
            jnp.concatenate([ci, ci, ci, ci], axis=1), jnp.concatenate([-si, si, -si, si], axis=1))


def _dsa_rope(qkv, kw, t_rows=256):
    s = qkv.shape[0]
    t_rows = min(t_rows, s)
    dq = ATT_HEADS * ATT_HEAD_DIM
    dkv = ATT_KV_HEADS * ATT_HEAD_DIM
    dqi = IDX_HEADS * IDX_DIM
    cos, sin, cosi, sini = _rope_tables(s)
    tile = lambda width: pl.BlockSpec((t_rows, width), lambda t: (t, 0))
    return pl.pallas_call(
        functools.partial(_rope_kernel, q_scale=ATT_HEAD_DIM ** -0.5 * math.log2(math.e),
                          w_scale=IDX_HEADS ** -0.5 * IDX_DIM ** -0.5),
        grid=(s // t_rows,),
        in_specs=[tile(qkv.shape[1]), tile(LANES), tile(LANES), tile(LANES), tile(LANES), tile(LANES)],
        out_specs=[tile(dq), tile(dkv), tile(dkv), tile(dqi), tile(LANES), tile(LANES)],
        out_shape=[
            jax.ShapeDtypeStruct((s, dq), BF16),
            jax.ShapeDtypeStruct((s, dkv), BF16),
            jax.ShapeDtypeStruct((s, dkv), BF16),
            jax.ShapeDtypeStruct((s, dqi), BF16),
            jax.ShapeDtypeStruct((s, LANES), BF16),
            jax.ShapeDtypeStruct((s, LANES), F32),
        ],
        compiler_params=_cparams(1),
        name="dsa_rope",
    )(qkv, kw, cos, sin, cosi, sini)


DSA_TQ = 128
DSA_KB = 512


def _sortable_key(score):
    bits = lax.bitcast_convert_type(score, jnp.int32)
    return jnp.where(bits < 0, bits ^ jnp.int32(0x7FFFFFFF), bits)


def _tile_lanes(v, width):
    return jnp.concatenate([v] * (width // v.shape[1]), axis=1)


def _dsa_kernel(q_ref, qi_ref, w_ref, k_ref, v_ref, ki_ref, o_ref,
                sc, qa, wb, qs, m_scr, acc_scr, *, top_k):
    i = pl.program_id(0)
    tq = q_ref.shape[0]
    kb_size = sc.shape[2]
    nkb = ((i + 1) * tq + kb_size - 1) // kb_size
    lane = lax.broadcasted_iota(jnp.int32, (tq, LANES), 1)
    chunk_shift = CHUNK.bit_length() - 1
    q_chunk = (i * tq + lax.broadcasted_iota(jnp.int32, (tq, kb_size), 0)) >> chunk_shift
    k_lane = lax.broadcasted_iota(jnp.int32, (tq, kb_size), 1)

    def admissible(kb):
        return ((kb * kb_size + k_lane) >> chunk_shift) <= q_chunk

    for h in range(IDX_HEADS):
        pair = qi_ref[:, (h // 2) * LANES:(h // 2 + 1) * LANES]
        keep = (lane < IDX_DIM) if h % 2 == 0 else (lane >= IDX_DIM)
        qa[h] = jnp.where(keep, pair, jnp.zeros_like(pair))
        wb[h] = jnp.broadcast_to(w_ref[:, h:h + 1], (tq, LANES))

    def score_block(kb, carry):
        kis = ki_ref[pl.ds(pl.multiple_of(kb * kb_size, kb_size), kb_size), :]
        score = jnp.zeros((tq, kb_size), F32)
        for h in range(IDX_HEADS):
            lg = _dot_nt(qa[h], kis)
            score = score + jnp.maximum(lg, 0.0) * _tile_lanes(wb[h], kb_size)
        sc[kb] = jnp.where(admissible(kb), _sortable_key(score), jnp.int32(KEY_NEG_INF))
        return carry

    lax.fori_loop(0, nkb, score_block, 0)

    @pl.when(nkb % 2 == 1)
    def _():
        sc[nkb] = jnp.full((tq, kb_size), KEY_NEG_INF, jnp.int32)

    def bit_step(b, thr):
        cand = thr + jnp.left_shift(jnp.int32(1), 31 - b)
        cand_t = _tile_lanes(cand, kb_size)

        def count_pair(pi, acc):
            for u in range(2):
                ge = jnp.where(sc[2 * pi + u] >= cand_t, 1.0, 0.0)
                for c in range(kb_size // LANES):
                    acc = acc + ge[:, c * LANES:(c + 1) * LANES]
            return acc

        acc = lax.fori_loop(0, (nkb + 1) // 2, count_pair, jnp.zeros((tq, LANES), F32))
        cnt = jnp.broadcast_to(jnp.sum(acc, axis=1, keepdims=True), (tq, LANES))
        return jnp.where(cnt >= float(top_k), cand, thr)

    thr = lax.fori_loop(0, 32, bit_step,
                        jnp.full((tq, LANES), jnp.iinfo(jnp.int32).min, jnp.int32))
    thr_t = _tile_lanes(thr, kb_size)

    def bias_block(kb, carry):
        sel = jnp.logical_and(sc[kb] >= thr_t, admissible(kb))
        sc[kb] = jnp.where(sel, jnp.int32(0), jnp.int32(NEG_INF_BITS))
        return carry

    lax.fori_loop(0, nkb, bias_block, 0)

    for g in range(ATT_KV_HEADS):
        qs[g] = jnp.concatenate(
            [q_ref[:, (g * ATT_GROUP + a) * LANES:(g * ATT_GROUP + a + 1) * LANES]
             for a in range(ATT_GROUP)], axis=0)
    m_scr[...] = jnp.full(m_scr.shape, NEG_INF, F32)
    acc_scr[...] = jnp.zeros_like(acc_scr)
    ones = jnp.ones((kb_size, LANES), BF16)

    def attn_block(kb, carry):
        start = pl.multiple_of(kb * kb_size, kb_size)
        bias = lax.bitcast_convert_type(sc[kb], F32)
        bias = jnp.concatenate([bias] * ATT_GROUP, axis=0)
        for g in range(ATT_KV_HEADS):
            kblk = k_ref[pl.ds(start, kb_size), g * LANES:(g + 1) * LANES]
            vblk = v_ref[pl.ds(start, kb_size), g * LANES:(g + 1) * LANES]
            s = _dot_nt(qs[g], kblk) + bias
            m_old = m_scr[g]
            m_new = jnp.maximum(m_old, jnp.max(s, axis=1, keepdims=True))
            m_safe = jnp.where(m_new == NEG_INF, 0.0, m_new)
            alpha = jnp.exp2(m_old - m_safe)
            p = jnp.exp2(s - _tile_lanes(m_safe, kb_size))
            pv = _dot(p.astype(BF16), jnp.concatenate([vblk, ones], axis=1))
            acc_scr[g] = _tile_lanes(alpha, 2 * LANES) * acc_scr[g] + pv
            m_scr[g] = m_new
        return carry

    lax.fori_loop(0, nkb, attn_block, 0)

    for g in range(ATT_KV_HEADS):
        out = acc_scr[g][:, :LANES] / acc_scr[g][:, LANES:]
        for a in range(ATT_GROUP):
            hh = g * ATT_GROUP + a
            o_ref[:, hh * LANES:(hh + 1) * LANES] = out[a * tq:(a + 1) * tq].astype(o_ref.dtype)


def _dsa_attend(q, k, v, qi, ki, w):
    s = q.shape[0]
    tq = min(DSA_TQ, s)
    kb = min(DSA_KB, s)
    top_k = min(TOPK_MAX, s // 4)
    resident = lambda width: pl.BlockSpec((s, width), lambda i: (0, 0),
                                          pipeline_mode=pl.Buffered(1))
    tile = lambda width: pl.BlockSpec((tq, width), lambda i: (i, 0))
    return pl.pallas_call(
        functools.partial(_dsa_kernel, top_k=top_k),
        grid=(s // tq,),
        in_specs=[tile(q.shape[1]), tile(qi.shape[1]), tile(LANES),
                  resident(k.shape[1]), resident(v.shape[1]), resident(LANES)],
        out_specs=tile(q.shape[1]),
        out_shape=jax.ShapeDtypeStruct(q.shape, BF16),
        scratch_shapes=[
            pltpu.VMEM((2 * pl.cdiv(s // kb, 2), tq, kb), jnp.int32),
            pltpu.VMEM((IDX_HEADS, tq, LANES), BF16),
            pltpu.VMEM((IDX_HEADS, tq, LANES), F32),
            pltpu.VMEM((ATT_KV_HEADS, ATT_GROUP * tq, LANES), BF16),
            pltpu.VMEM((ATT_KV_HEADS, ATT_GROUP * tq, LANES), F32),
            pltpu.VMEM((ATT_KV_HEADS, ATT_GROUP * tq, 2 * LANES), F32),
        ],
        compiler_params=_cparams(1),
        name="dsa_attend",
    )(q, qi, w, k, v, ki)


def _dsa_layer(h, mix_norm, w_in, w_out):
    n_main = (ATT_HEADS + 2 * ATT_KV_HEADS) * ATT_HEAD_DIM + IDX_HEADS * IDX_DIM
    n_rest = w_in.shape[1] - n_main
    w_rest = jnp.pad(w_in[:, n_main:], ((0, 0), (0, LANES - n_rest)))
    qkv = _matmul(h, w_in, norm_g=mix_norm, out_dtype=F32, n_cols=n_main, name="dsa_in")
    kw = _matmul(h, w_rest, norm_g=mix_norm, out_dtype=F32, name="dsa_in_idx")
    q, k, v, qi, ki, w = _dsa_rope(qkv, kw)
    o = _dsa_attend(q, k, v, qi, ki, w)
    return _matmul(o, w_out, residual=h, out_dtype=F32, name="dsa_out")


def _mlp_kernel(x_ref, g_ref, wu_ref, wd_ref, o_ref, xn_ref):
    f = pl.program_id(1)

    @pl.when(f == 0)
    def _():
        xf = x_ref[...]
        ms = jnp.mean(xf * xf, axis=-1, keepdims=True)
        xn_ref[...] = (xf * lax.rsqrt(ms + RMS_EPS) * g_ref[...]).astype(BF16)
        o_ref[...] = xf

    u = jnp.maximum(_dot(xn_ref[...], wu_ref[...].astype(BF16)), 0.0)
    o_ref[...] += _dot((u * u).astype(BF16), wd_ref[...].astype(BF16))


def _mlp(h, norm_g, w_up, w_down, tm=1024, tf=512):
    m, d = h.shape
    dff = w_up.shape[1]
    tm = min(tm, m)
    tf = min(tf, dff)
    return pl.pallas_call(
        _mlp_kernel,
        grid=(m // tm, dff // tf),
        in_specs=[
            pl.BlockSpec((tm, d), lambda i, f: (i, 0), pipeline_mode=pl.Buffered(1)),
            pl.BlockSpec((1, d), lambda i, f: (0, 0)),
            pl.BlockSpec((d, tf), lambda i, f: (0, f)),
            pl.BlockSpec((tf, d), lambda i, f: (f, 0)),
        ],
        out_specs=pl.BlockSpec((tm, d), lambda i, f: (i, 0)),
        out_shape=jax.ShapeDtypeStruct((m, d), F32),
        scratch_shapes=[pltpu.VMEM((tm, d), BF16)],
        compiler_params=_cparams(2),
        name="mlp",
    )(h, norm_g.reshape(1, d).astype(F32), w_up, w_down)


def kernel(x, l0_mix_norm, l0_rg_in_w, l0_rg_conv_w, l0_rg_conv_b, l0_rg_wa, l0_rg_ba, l0_rg_wx, l0_rg_bx, l0_rg_lambda, l0_rg_out_w, l0_mlp_norm, l0_mlp_up, l0_mlp_down, l1_mix_norm, l1_dsa_in_w, l1_dsa_out_w, l1_mlp_norm, l1_mlp_up, l1_mlp_down, l2_mix_norm, l2_ssd_in_w, l2_ssd_conv_w, l2_ssd_conv_b, l2_ssd_dt_bias, l2_ssd_a_log, l2_ssd_d, l2_ssd_norm, l2_ssd_out_w, l2_mlp_norm, l2_mlp_up, l2_mlp_down, l3_mix_norm, l3_rg_in_w, l3_rg_conv_w, l3_rg_conv_b, l3_rg_wa, l3_rg_ba, l3_rg_wx, l3_rg_bx, l3_rg_lambda, l3_rg_out_w, l3_mlp_norm, l3_mlp_up, l3_mlp_down, final_norm):
    bsz, seq, d = x.shape
    outs = []
    for b in range(bsz):
        h = x[b]
        h = _rglru_layer(h, l0_mix_norm, l0_rg_in_w, l0_rg_conv_w, l0_rg_conv_b, l0_rg_wa,
                         l0_rg_ba, l0_rg_wx, l0_rg_bx, l0_rg_lambda, l0_rg_out_w)
        h = _mlp(h, l0_mlp_norm, l0_mlp_up, l0_mlp_down)
        h = _dsa_layer(h, l1_mix_norm, l1_dsa_in_w, l1_dsa_out_w)
        h = _mlp(h, l1_mlp_norm, l1_mlp_up, l1_mlp_down)
        h = _ssd_layer(h, l2_mix_norm, l2_ssd_in_w, l2_ssd_conv_w, l2_ssd_conv_b, l2_ssd_dt_bias,
                       l2_ssd_a_log, l2_ssd_d, l2_ssd_norm, l2_ssd_out_w)
        h = _mlp(h, l2_mlp_norm, l2_mlp_up, l2_mlp_down)
        h = _rglru_layer(h, l3_mix_norm, l3_rg_in_w, l3_rg_conv_w, l3_rg_conv_b, l3_rg_wa,
                         l3_rg_ba, l3_rg_wx, l3_rg_bx, l3_rg_lambda, l3_rg_out_w)
        h = _mlp(h, l3_mlp_norm, l3_mlp_up, l3_mlp_down)
        outs.append(_rmsnorm(h, final_norm))
    return jnp.stack(outs, axis=0)
```

```python
import functools
import math

import jax
import jax.numpy as jnp
from jax import lax
from jax.experimental import pallas as pl
from jax.experimental.pallas import tpu as pltpu

F32 = jnp.float32
BF16 = jnp.bfloat16

RMS_EPS = 1e-6
ROPE_THETA = 10000.0
CHUNK = 64
RG_BLOCKS = 8
RG_CONV = 4
RG_C = 8.0
ATT_HEADS = 16
ATT_KV_HEADS = 4
ATT_HEAD_DIM = 128
ATT_GROUP = ATT_HEADS // ATT_KV_HEADS
IDX_HEADS = 16
IDX_DIM = 64
TOPK_MAX = 256
SSD_HEAD_DIM = 64
SSD_GROUPS = 8
SSD_STATE = 128
SSD_CONV = 4

V7X_VMEM_BYTES = 64 * 1024 * 1024
VMEM_LIMIT_BYTES = V7X_VMEM_BYTES - 8 * 1024 * 1024
LANES = 128
SUBLANES = 8

NEG_INF = float("-inf")
KEY_NEG_INF = -2139095041
NEG_INF_BITS = -8388608


def _cparams(n_axes):
    return pltpu.CompilerParams(
        dimension_semantics=("arbitrary",) * n_axes,
        vmem_limit_bytes=VMEM_LIMIT_BYTES,
    )


def _dot(a, b):
    return jnp.dot(a, b, preferred_element_type=F32)


def _dot_nt(a, b):
    return lax.dot_general(a, b, (((1,), (1,)), ((), ())), preferred_element_type=F32)


def _dot_tn(a, b):
    return lax.dot_general(a, b, (((0,), (0,)), ((), ())), preferred_element_type=F32)


def _softplus(x):
    return jnp.maximum(x, 0.0) + jnp.log1p(jnp.exp(-jnp.abs(x)))


def _sigmoid(x):
    return jax.nn.sigmoid(x)


def _mm_kernel(*refs, has_norm, act, has_res, nk):
    it = iter(refs)
    x_ref = next(it)
    g_ref = next(it) if has_norm else None
    w_ref = next(it)
    r_ref = next(it) if has_res else None
    o_ref = next(it)
    acc_ref = next(it) if nk > 1 else None
    xn_ref = next(it) if has_norm else None
    j = pl.program_id(1)
    k = pl.program_id(2)

    if has_norm:
        @pl.when(j == 0)
        def _():
            xf = x_ref[...].astype(F32)
            ms = jnp.mean(xf * xf, axis=-1, keepdims=True)
            xn_ref[...] = (xf * lax.rsqrt(ms + RMS_EPS) * g_ref[...]).astype(BF16)

        lhs = xn_ref[...]
    else:
        lhs = x_ref[...].astype(BF16)

    p = _dot(lhs, w_ref[...].astype(BF16))

    def finish(v):
        if act == "relu2":
            v = jnp.maximum(v, 0.0)
            v = v * v
        if has_res:
            v = v + r_ref[...]
        o_ref[...] = v.astype(o_ref.dtype)

    if nk == 1:
        finish(p)
    else:
        @pl.when(k == 0)
        def _():
            acc_ref[...] = p

        @pl.when(jnp.logical_and(k > 0, k < nk - 1))
        def _():
            acc_ref[...] += p

        @pl.when(k == nk - 1)
        def _():
            finish(acc_ref[...] + p)


def _matmul(x, w, *, norm_g=None, act=None, residual=None, out_dtype=F32,
            tm=2048, tn=512, tk=None, n_cols=None, name="matmul"):
    m, kdim = x.shape
    n = w.shape[1] if n_cols is None else n_cols
    tm = min(tm, m)
    tn = min(tn, n)
    tk = kdim if tk is None else min(tk, kdim)
    has_norm = norm_g is not None
    if has_norm:
        assert tk == kdim
    assert m % tm == 0 and n % tn == 0 and kdim % tk == 0
    nk = kdim // tk
    grid = (m // tm, n // tn, nk)

    x_mode = {"pipeline_mode": pl.Buffered(1)} if x.dtype == F32 and nk == 1 else {}
    in_specs = [pl.BlockSpec((tm, tk), lambda i, j, k: (i, k), **x_mode)]
    args = [x]
    if has_norm:
        in_specs.append(pl.BlockSpec((1, tk), lambda i, j, k: (0, 0)))
        args.append(norm_g.reshape(1, kdim).astype(F32))
    in_specs.append(pl.BlockSpec((tk, tn), lambda i, j, k: (k, j)))
    args.append(w)
    if residual is not None:
        in_specs.append(pl.BlockSpec((tm, tn), lambda i, j, k: (i, j)))
        args.append(residual)
    scratch = []
    if nk > 1:
        scratch.append(pltpu.VMEM((tm, tn), F32))
    if has_norm:
        scratch.append(pltpu.VMEM((tm, tk), BF16))

    return pl.pallas_call(
        functools.partial(_mm_kernel, has_norm=has_norm, act=act,
                          has_res=residual is not None, nk=nk),
        grid=grid,
        in_specs=in_specs,
        out_specs=pl.BlockSpec((tm, tn), lambda i, j, k: (i, j)),
        out_shape=jax.ShapeDtypeStruct((m, n), out_dtype),
        scratch_shapes=scratch,
        compiler_params=_cparams(3),
        name=name,
    )(*args)


def _rmsnorm_kernel(x_ref, g_ref, o_ref):
    xf = x_ref[...]
    ms = jnp.mean(xf * xf, axis=-1, keepdims=True)
    o_ref[...] = xf * lax.rsqrt(ms + RMS_EPS) * g_ref[...]


def _rmsnorm(x, g, tm=512):
    m, d = x.shape
    tm = min(tm, m)
    return pl.pallas_call(
        _rmsnorm_kernel,
        grid=(m // tm,),
        in_specs=[pl.BlockSpec((tm, d), lambda i: (i, 0)),
                  pl.BlockSpec((1, d), lambda i: (0, 0))],
        out_specs=pl.BlockSpec((tm, d), lambda i: (i, 0)),
        out_shape=jax.ShapeDtypeStruct((m, d), F32),
        compiler_params=_cparams(1),
        name="final_norm",
    )(x, g.reshape(1, d))


HALO = SUBLANES


def _causal_conv(buf, x_ref, cw_ref, cb_ref, n_taps):
    t_rows = x_ref.shape[0]
    buf[HALO:HALO + t_rows, :] = x_ref[...]
    out = cb_ref[...]
    for k in range(n_taps):
        off = HALO - (n_taps - 1) + k
        out = out + cw_ref[k:k + 1, :] * buf[off:off + t_rows, :]
    buf[0:HALO, :] = buf[t_rows:t_rows + HALO, :]
    return out


def _rg_kernel(gate_ref, x_ref, cw_ref, cb_ref, wa_ref, wx_ref, ba_ref, bx_ref,
               lam_ref, y_ref, xbuf, a_scr, b_scr, hcar):
    t = pl.program_id(0)
    t_rows, width = x_ref.shape
    n_blocks, bw, _ = wa_ref.shape

    @pl.when(t == 0)
    def _():
        xbuf[0:HALO, :] = jnp.zeros((HALO, width), F32)
        hcar[...] = jnp.zeros_like(hcar)

    xc = _causal_conv(xbuf, x_ref, cw_ref, cb_ref, RG_CONV)
    xcb = xc.astype(BF16)
    sp = _softplus(-lam_ref[...])
    for c in range(n_blocks):
        sl = slice(c * bw, (c + 1) * bw)
        xs = xcb[:, sl]
        r = _sigmoid(_dot(xs, wa_ref[c]) + ba_ref[:, sl])
        i = _sigmoid(_dot(xs, wx_ref[c]) + bx_ref[:, sl])
        log_a = -RG_C * r * sp[:, sl]
        th = jnp.tanh(log_a)
        mult = jnp.sqrt(-2.0 * th / (1.0 - th))
        a_scr[:, sl] = jnp.exp(log_a)
        b_scr[:, sl] = mult * (i * xc[:, sl])

    def step(r, h):
        h = a_scr[pl.ds(r, 1), :] * h + b_scr[pl.ds(r, 1), :]
        b_scr[pl.ds(r, 1), :] = h
        return h

    hcar[...] = lax.fori_loop(0, t_rows, step, hcar[...], unroll=8)
    y_ref[...] = (b_scr[...] * jax.nn.gelu(gate_ref[...])).astype(y_ref.dtype)


def _rg_core(xg, conv_w, conv_b, wa, ba, wx, bx, lam, t_rows=256):
    s, two_w = xg.shape
    width = two_w // 2
    t_rows = min(t_rows, s)
    row = lambda v: v.reshape(1, width).astype(F32)
    full2 = lambda shape: pl.BlockSpec(shape, lambda t: (0, 0))
    full3 = lambda shape: pl.BlockSpec(shape, lambda t: (0, 0, 0))
    return pl.pallas_call(
        _rg_kernel,
        grid=(s // t_rows,),
        in_specs=[
            pl.BlockSpec((t_rows, width), lambda t: (t, 0)),
            pl.BlockSpec((t_rows, width), lambda t: (t, 1)),
            full2((RG_CONV, width)), full2((1, width)),
            full3(wa.shape), full3(wx.shape),
            full2((1, width)), full2((1, width)), full2((1, width)),
        ],
        out_specs=pl.BlockSpec((t_rows, width), lambda t: (t, 0)),
        out_shape=jax.ShapeDtypeStruct((s, width), BF16),
        scratch_shapes=[
            pltpu.VMEM((t_rows + HALO, width), F32),
            pltpu.VMEM((t_rows, width), F32),
            pltpu.VMEM((t_rows, width), F32),
            pltpu.VMEM((1, width), F32),
        ],
        compiler_params=_cparams(1),
        name="rg_core",
    )(xg, xg, conv_w.astype(F32), row(conv_b), wa.astype(BF16), wx.astype(BF16),
      row(ba), row(bx), row(lam))


def _rglru_layer(h, mix_norm, w_in, conv_w, conv_b, w_a, b_a, w_x, b_x, lam, w_out):
    xg = _matmul(h, w_in, norm_g=mix_norm, out_dtype=F32, name="rg_in")
    y = _rg_core(xg, conv_w, conv_b, w_a, b_a, w_x, b_x, lam)
    return _matmul(y, w_out, residual=h, out_dtype=F32, name="rg_out")


SSD_TILE = 128
SSD_HPG = 8
SSD_GW = SSD_HPG * SSD_HEAD_DIM
SSD_GPS = 2


def _cumsum(v, axis):
    n = v.shape[axis]
    idx = lax.broadcasted_iota(jnp.int32, v.shape, axis)
    shift = 1
    while shift < n:
        v = v + jnp.where(idx >= shift, pltpu.roll(v, shift, axis), 0.0)
        shift *= 2
    return v


def _expand_heads(cols, base):
    rows = cols.shape[0]
    lane = lax.broadcasted_iota(jnp.int32, (rows, LANES), 1)
    parts = []
    for v in range(SSD_GW // LANES):
        lo = jnp.broadcast_to(cols[:, base + 2 * v:base + 2 * v + 1], (rows, LANES))
        hi = jnp.broadcast_to(cols[:, base + 2 * v + 1:base + 2 * v + 2], (rows, LANES))
        parts.append(jnp.where(lane < SSD_HEAD_DIM, lo, hi))
    return jnp.concatenate(parts, axis=1)


def _ssd_kernel(z_ref, x_ref, b_ref, c_ref, cwx_ref, cwb_ref, cwc_ref, cbx_ref,
                cbb_ref, cbc_ref, dtr_ref, dbr_ref, alr_ref, dsk_ref, ng_ref,
                y_ref, xbuf, bbuf, cbuf, st):
    t = pl.program_id(1)
    rows = x_ref.shape[0]
    hpg, gw, ns = SSD_HPG, SSD_GW, SSD_STATE

    @pl.when(t == 0)
    def _():
        xbuf[0:HALO, :] = jnp.zeros((HALO, xbuf.shape[1]), F32)
        bbuf[0:HALO, :] = jnp.zeros((HALO, bbuf.shape[1]), F32)
        cbuf[0:HALO, :] = jnp.zeros((HALO, cbuf.shape[1]), F32)
        st[...] = jnp.zeros_like(st)

    def conv_silu(buf, ref, cw, cb):
        v = _causal_conv(buf, ref, cw, cb, SSD_CONV)
        return v * _sigmoid(v)

    xs_all = conv_silu(xbuf, x_ref, cwx_ref, cbx_ref)
    bm_all = conv_silu(bbuf, b_ref, cwb_ref, cbb_ref)
    cm_all = conv_silu(cbuf, c_ref, cwc_ref, cbc_ref)

    li = lax.broadcasted_iota(jnp.int32, (rows, rows), 0)
    si = lax.broadcasted_iota(jnp.int32, (rows, rows), 1)
    causal = li >= si
    eye = jnp.where(li == si, 1.0, 0.0)
    lane = lax.broadcasted_iota(jnp.int32, (rows, LANES), 1)

    for gi in range(x_ref.shape[1] // gw):
        gsl = slice(gi * gw, (gi + 1) * gw)
        xs = xs_all[:, gsl]
        bmb = bm_all[:, gi * ns:(gi + 1) * ns].astype(BF16)
        cmb = cm_all[:, gi * ns:(gi + 1) * ns].astype(BF16)
        hs = slice(gi * hpg, (gi + 1) * hpg)

        dtr = _softplus(dtr_ref[hs, :] + dbr_ref[hs, :])
        csr = _cumsum(dtr * (-jnp.exp(alr_ref[hs, :])), 1)
        both = lax.dot_general(eye, jnp.concatenate([dtr, csr], axis=0),
                               (((1,), (1,)), ((), ())),
                               precision=lax.Precision.HIGHEST,
                               preferred_element_type=F32)
        last = both[rows - 1:rows, :]

        cb = _dot_nt(cmb, bmb)
        xdt = xs * _expand_heads(both, 0)
        xdtb = xdt.astype(BF16)
        ys = []
        for v in range(gw // LANES):
            xv = xdtb[:, v * LANES:(v + 1) * LANES]
            pair = []
            for j in (2 * v, 2 * v + 1):
                seg = both[:, hpg + j:hpg + j + 1] - csr[j:j + 1, :]
                dec = jnp.exp(jnp.where(causal, seg, NEG_INF))
                pair.append(_dot((cb * dec).astype(BF16), xv))
            ys.append(jnp.where(lane < SSD_HEAD_DIM, pair[0], pair[1]))
        y = jnp.concatenate(ys, axis=1)
        stg = st[:, gsl]
        y = y + _dot(cmb, stg.astype(BF16)) * _expand_heads(jnp.exp(both), hpg)
        y = y + xs * dsk_ref[:, gsl]

        z = z_ref[:, gsl]
        yz = y * (z * _sigmoid(z))
        yz = yz * lax.rsqrt(jnp.mean(yz * yz, axis=-1, keepdims=True) + RMS_EPS)
        y_ref[:, gsl] = (yz * ng_ref[:, gsl]).astype(y_ref.dtype)

        xw = (xdt * _expand_heads(jnp.exp(last - both), hpg)).astype(BF16)
        st[:, gsl] = stg * _expand_heads(jnp.exp(last), hpg) + _dot_tn(bmb, xw)


def _ssd_core(zx, dt_raw, conv_w, conv_b, dt_bias, a_log, d_skip, norm_g):
    s = zx.shape[0]
    g = SSD_GROUPS
    gps = SSD_GPS
    inner = g * SSD_GW
    rows = min(SSD_TILE, s)
    xw, bw = gps * SSD_GW, gps * SSD_STATE
    nx = inner // xw
    nb = 2 * inner // bw
    nc = nb + g * SSD_STATE // bw
    cwb0 = inner // bw
    cwc0 = cwb0 + g * SSD_STATE // bw
    hps = gps * SSD_HPG

    dt_t = jnp.transpose(dt_raw)
    colv = lambda v: jnp.pad(v.astype(F32), (0, LANES - v.shape[0])).reshape(LANES, 1)
    conv_w = conv_w.astype(F32)
    conv_b = conv_b.reshape(1, -1).astype(F32)
    dsk = jnp.repeat(d_skip.astype(F32), SSD_HEAD_DIM).reshape(1, inner)
    ng = norm_g.reshape(1, inner).astype(F32)

    gs = lambda shape, fn: pl.BlockSpec(shape, fn)
    in_specs = [
        gs((rows, xw), lambda gi, t: (t, gi)),
        gs((rows, xw), lambda gi, t: (t, nx + gi)),
        gs((rows, bw), lambda gi, t: (t, nb + gi)),
        gs((rows, bw), lambda gi, t: (t, nc + gi)),
        gs((SSD_CONV, xw), lambda gi, t: (0, gi)),
        gs((SSD_CONV, bw), lambda gi, t: (0, cwb0 + gi)),
        gs((SSD_CONV, bw), lambda gi, t: (0, cwc0 + gi)),
        gs((1, xw), lambda gi, t: (0, gi)),
        gs((1, bw), lambda gi, t: (0, cwb0 + gi)),
        gs((1, bw), lambda gi, t: (0, cwc0 + gi)),
        gs((hps, rows), lambda gi, t: (gi, t)),
        gs((hps, 1), lambda gi, t: (gi, 0)),
        gs((hps, 1), lambda gi, t: (gi, 0)),
        gs((1, xw), lambda gi, t: (0, gi)),
        gs((1, xw), lambda gi, t: (0, gi)),
    ]
    return pl.pallas_call(
        _ssd_kernel,
        grid=(g // gps, s // rows),
        in_specs=in_specs,
        out_specs=pl.BlockSpec((rows, xw), lambda gi, t: (t, gi)),
        out_shape=jax.ShapeDtypeStruct((s, inner), BF16),
        scratch_shapes=[
            pltpu.VMEM((rows + HALO, xw), F32),
            pltpu.VMEM((rows + HALO, bw), F32),
            pltpu.VMEM((rows + HALO, bw), F32),
            pltpu.VMEM((SSD_STATE, xw), F32),
        ],
        compiler_params=_cparams(2),
        name="ssd_core",
    )(zx, zx, zx, zx, conv_w, conv_w, conv_w, conv_b, conv_b, conv_b,
      dt_t, colv(dt_bias), colv(a_log), dsk, ng)


def _ssd_layer(h, mix_norm, w_in, conv_w, conv_b, dt_bias, a_log, d_skip, norm_g, w_out):
    heads = dt_bias.shape[0]
    n_zx = w_in.shape[1] - heads
    w_dt = jnp.pad(w_in[:, n_zx:], ((0, 0), (0, LANES - heads)))
    zx = _matmul(h, w_in, norm_g=mix_norm, out_dtype=F32, n_cols=n_zx, name="ssd_in")
    dt_raw = _matmul(h, w_dt, norm_g=mix_norm, out_dtype=F32, name="ssd_in_dt")
    y = _ssd_core(zx, dt_raw, conv_w, conv_b, dt_bias, a_log, d_skip, norm_g)
    return _matmul(y, w_out, residual=h, out_dtype=F32, tm=1024, name="ssd_out")


def _rope_kernel(qkv_ref, kw_ref, cos_ref, sin_ref, cosi_ref, sini_ref,
                 q_ref, k_ref, v_ref, qi_ref, ki_ref, w_ref, *, q_scale, w_scale):
    rows = qkv_ref.shape[0]
    lane = lax.broadcasted_iota(jnp.int32, (rows, LANES), 1)
    cos, sin = cos_ref[...], sin_ref[...]
    cosi, sini = cosi_ref[...], sini_ref[...]
    half_i = IDX_DIM // 2
    lower_i = (lane & (IDX_DIM - 1)) < half_i

    def rope_head(x):
        return x * cos + pltpu.roll(x, ATT_HEAD_DIM // 2, 1) * sin

    def rope_idx(x):
        rot = jnp.where(lower_i, pltpu.roll(x, LANES - half_i, 1), pltpu.roll(x, half_i, 1))
        return x * cosi + rot * sini

    nq = q_ref.shape[1] // LANES
    nkv = k_ref.shape[1] // LANES
    nqi = qi_ref.shape[1] // LANES
    for hh in range(nq):
        x = qkv_ref[:, hh * LANES:(hh + 1) * LANES]
        q_ref[:, hh * LANES:(hh + 1) * LANES] = (rope_head(x) * q_scale).astype(q_ref.dtype)
    off = nq
    for hh in range(nkv):
        x = qkv_ref[:, (off + hh) * LANES:(off + hh + 1) * LANES]
        k_ref[:, hh * LANES:(hh + 1) * LANES] = rope_head(x).astype(k_ref.dtype)
    off += nkv
    v_ref[...] = qkv_ref[:, off * LANES:(off + nkv) * LANES].astype(v_ref.dtype)
    off += nkv
    for hh in range(nqi):
        x = qkv_ref[:, (off + hh) * LANES:(off + hh + 1) * LANES]
        qi_ref[:, hh * LANES:(hh + 1) * LANES] = rope_idx(x).astype(qi_ref.dtype)
    kw = kw_ref[...]
    swapped = pltpu.roll(kw, LANES // 2, 1)
    ki_ref[...] = rope_idx(jnp.where(lane < IDX_DIM, kw, swapped)).astype(ki_ref.dtype)
    w_ref[...] = jnp.where(lane < IDX_HEADS, swapped * w_scale, 0.0)


def _rope_tables(seq):
    def table(dim):
        inv = ROPE_THETA ** (-jnp.arange(0, dim, 2, dtype=F32) / dim)
        ang = jnp.arange(seq, dtype=F32)[:, None] * inv[None, :]
        return jnp.cos(ang), jnp.sin(ang)

    c, s = table(ATT_HEAD_DIM)
    ci, si = table(IDX_DIM)
    return (jnp.concatenate([c, c], axis=1), jnp.concatenate([-s, s], axis=1),
            jnp.concatenate([ci, ci, ci, ci], axis=1), jnp.concatenate([-si, si, -si, si], axis=1))


def _dsa_rope(qkv, kw, t_rows=256):
    s = qkv.shape[0]
    t_rows = min(t_rows, s)
    dq = ATT_HEADS * ATT_HEAD_DIM
    dkv = ATT_KV_HEADS * ATT_HEAD_DIM
    dqi = IDX_HEADS * IDX_DIM
    cos, sin, cosi, sini = _rope_tables(s)
    tile = lambda width: pl.BlockSpec((t_rows, width), lambda t: (t, 0))
    return pl.pallas_call(
        functools.partial(_rope_kernel, q_scale=ATT_HEAD_DIM ** -0.5 * math.log2(math.e),
                          w_scale=IDX_HEADS ** -0.5 * IDX_DIM ** -0.5),
        grid=(s // t_rows,),
        in_specs=[tile(qkv.shape[1]), tile(LANES), tile(LANES), tile(LANES), tile(LANES), tile(LANES)],
        out_specs=[tile(dq), tile(dkv), tile(dkv), tile(dqi), tile(LANES), tile(LANES)],
        out_shape=[
            jax.ShapeDtypeStruct((s, dq), BF16),
            jax.ShapeDtypeStruct((s, dkv), BF16),
            jax.ShapeDtypeStruct((s, dkv), BF16),
            jax.ShapeDtypeStruct((s, dqi), BF16),
            jax.ShapeDtypeStruct((s, LANES), BF16),
            jax.ShapeDtypeStruct((s, LANES), F32),
        ],
        compiler_params=_cparams(1),
        name="dsa_rope",
    )(qkv, kw, cos, sin, cosi, sini)


DSA_TQ = 128
DSA_KB = 512


def _sortable_key(score):
    bits = lax.bitcast_convert_type(score, jnp.int32)
    return jnp.where(bits < 0, bits ^ jnp.int32(0x7FFFFFFF), bits)


def _key_hi(key):
    return jnp.right_shift(key, 16).astype(jnp.int16)


def _key_lo(key):
    return ((key & 0xFFFF) - 32768).astype(jnp.int16)


def _tile_lanes(v, width):
    return jnp.concatenate([v] * (width // v.shape[1]), axis=1)


def _dsa_kernel(q_ref, qi_ref, w_ref, k_ref, v_ref, ki_ref, o_ref,
                sc, hi16, lo16, qa, wb, qs, m_scr, acc_scr, *, top_k):
    i = pl.program_id(0)
    tq = q_ref.shape[0]
    kb_size = sc.shape[2]
    nkb = ((i + 1) * tq + kb_size - 1) // kb_size
    lane = lax.broadcasted_iota(jnp.int32, (tq, LANES), 1)
    chunk_shift = CHUNK.bit_length() - 1
    q_chunk = (i * tq + lax.broadcasted_iota(jnp.int32, (tq, kb_size), 0)) >> chunk_shift
    k_lane = lax.broadcasted_iota(jnp.int32, (tq, kb_size), 1)

    def admissible(kb):
        return ((kb * kb_size + k_lane) >> chunk_shift) <= q_chunk

    for h in range(IDX_HEADS):
        pair = qi_ref[:, (h // 2) * LANES:(h // 2 + 1) * LANES]
        keep = (lane < IDX_DIM) if h % 2 == 0 else (lane >= IDX_DIM)
        qa[h] = jnp.where(keep, pair, jnp.zeros_like(pair))
        wb[h] = jnp.broadcast_to(w_ref[:, h:h + 1], (tq, LANES))

    def score_block(kb, carry):
        kis = ki_ref[pl.ds(pl.multiple_of(kb * kb_size, kb_size), kb_size), :]
        score = jnp.zeros((tq, kb_size), F32)
        for h in range(IDX_HEADS):
            lg = _dot_nt(qa[h], kis)
            score = score + jnp.maximum(lg, 0.0) * _tile_lanes(wb[h], kb_size)
        key = jnp.where(admissible(kb), _sortable_key(score), jnp.int32(KEY_NEG_INF))
        sc[kb] = key
        hi16[kb] = _key_hi(key)
        lo16[kb] = _key_lo(key)
        return carry

    lax.fori_loop(0, nkb, score_block, 0)

    @pl.when(nkb % 2 == 1)
    def _():
        pad = jnp.full((tq, kb_size), KEY_NEG_INF, jnp.int32)
        hi16[nkb] = _key_hi(pad)
        lo16[nkb] = _key_lo(pad)

    n_pairs = (nkb + 1) // 2
    i16 = jnp.int16

    def count16(arr, cand, strict):
        cand_t = _tile_lanes(cand.astype(i16), kb_size)

        def count_pair(pi, acc):
            for u in range(2):
                blk = arr[2 * pi + u]
                hit = (blk > cand_t) if strict else (blk >= cand_t)
                one = jnp.where(hit, jnp.ones_like(blk), jnp.zeros_like(blk))
                for c in range(kb_size // LANES):
                    acc = acc + one[:, c * LANES:(c + 1) * LANES]
            return acc

        acc = lax.fori_loop(0, n_pairs, count_pair, jnp.zeros((tq, LANES), i16))
        cnt = jnp.sum(acc.astype(jnp.int32).astype(F32), axis=1, keepdims=True)
        return jnp.broadcast_to(cnt, (tq, LANES))

    def kth_largest16(arr, target):
        def bit_step(b, thr):
            cand = thr + jnp.left_shift(jnp.int32(1), 15 - b)
            return jnp.where(count16(arr, cand, False) >= target, cand, thr)

        return lax.fori_loop(0, 16, bit_step, jnp.full((tq, LANES), -32768, jnp.int32))

    k_f = jnp.full((tq, LANES), float(top_k), F32)
    thr_hi = kth_largest16(hi16, k_f)
    need = k_f - count16(hi16, thr_hi, True)
    thr_hi_t = _tile_lanes(thr_hi.astype(i16), kb_size)

    def tie_block(kb, carry):
        lo16[kb] = jnp.where(hi16[kb] == thr_hi_t, lo16[kb], jnp.full((tq, kb_size), -32768, i16))
        return carry

    lax.fori_loop(0, 2 * n_pairs, tie_block, 0)
    thr_lo = kth_largest16(lo16, need)
    thr = jnp.left_shift(thr_hi, 16) | ((thr_lo + 32768) & 0xFFFF)
    thr_t = _tile_lanes(thr, kb_size)

    def bias_block(kb, carry):
        sel = jnp.logical_and(sc[kb] >= thr_t, admissible(kb))
        sc[kb] = jnp.where(sel, jnp.int32(0), jnp.int32(NEG_INF_BITS))
        return carry

    lax.fori_loop(0, nkb, bias_block, 0)

    for g in range(ATT_KV_HEADS):
        qs[g] = jnp.concatenate(
            [q_ref[:, (g * ATT_GROUP + a) * LANES:(g * ATT_GROUP + a + 1) * LANES]
             for a in range(ATT_GROUP)], axis=0)
    m_scr[...] = jnp.full(m_scr.shape, NEG_INF, F32)
    acc_scr[...] = jnp.zeros_like(acc_scr)
    ones = jnp.ones((kb_size, LANES), BF16)

    def attn_block(kb, carry):
        start = pl.multiple_of(kb * kb_size, kb_size)
        bias = lax.bitcast_convert_type(sc[kb], F32)
        bias = jnp.concatenate([bias] * ATT_GROUP, axis=0)
        for g in range(ATT_KV_HEADS):
            kblk = k_ref[pl.ds(start, kb_size), g * LANES:(g + 1) * LANES]
            vblk = v_ref[pl.ds(start, kb_size), g * LANES:(g + 1) * LANES]
            s = _dot_nt(qs[g], kblk) + bias
            m_old = m_scr[g]
            m_new = jnp.maximum(m_old, jnp.max(s, axis=1, keepdims=True))
            m_safe = jnp.where(m_new == NEG_INF, 0.0, m_new)
            alpha = jnp.exp2(m_old - m_safe)
            p = jnp.exp2(s - _tile_lanes(m_safe, kb_size))
            pv = _dot(p.astype(BF16), jnp.concatenate([vblk, ones], axis=1))
            acc_scr[g] = _tile_lanes(alpha, 2 * LANES) * acc_scr[g] + pv
            m_scr[g] = m_new
        return carry

    lax.fori_loop(0, nkb, attn_block, 0)

    for g in range(ATT_KV_HEADS):
        out = acc_scr[g][:, :LANES] / acc_scr[g][:, LANES:]
        for a in range(ATT_GROUP):
            hh = g * ATT_GROUP + a
            o_ref[:, hh * LANES:(hh + 1) * LANES] = out[a * tq:(a + 1) * tq].astype(o_ref.dtype)


def _dsa_attend(q, k, v, qi, ki, w):
    s = q.shape[0]
    tq = min(DSA_TQ, s)
    kb = min(DSA_KB, s)
    top_k = min(TOPK_MAX, s // 4)
    resident = lambda width: pl.BlockSpec((s, width), lambda i: (0, 0),
                                          pipeline_mode=pl.Buffered(1))
    tile = lambda width: pl.BlockSpec((tq, width), lambda i: (i, 0))
    return pl.pallas_call(
        functools.partial(_dsa_kernel, top_k=top_k),
        grid=(s // tq,),
        in_specs=[tile(q.shape[1]), tile(qi.shape[1]), tile(LANES),
                  resident(k.shape[1]), resident(v.shape[1]), resident(LANES)],
        out_specs=tile(q.shape[1]),
        out_shape=jax.ShapeDtypeStruct(q.shape, BF16),
        scratch_shapes=[
            pltpu.VMEM((s // kb, tq, kb), jnp.int32),
            pltpu.VMEM((2 * pl.cdiv(s // kb, 2), tq, kb), jnp.int16),
            pltpu.VMEM((2 * pl.cdiv(s // kb, 2), tq, kb), jnp.int16),
            pltpu.VMEM((IDX_HEADS, tq, LANES), BF16),
            pltpu.VMEM((IDX_HEADS, tq, LANES), F32),
            pltpu.VMEM((ATT_KV_HEADS, ATT_GROUP * tq, LANES), BF16),
            pltpu.VMEM((ATT_KV_HEADS, ATT_GROUP * tq, LANES), F32),
            pltpu.VMEM((ATT_KV_HEADS, ATT_GROUP * tq, 2 * LANES), F32),
        ],
        compiler_params=_cparams(1),
        name="dsa_attend",
    )(q, qi, w, k, v, ki)


def _dsa_layer(h, mix_norm, w_in, w_out):
    n_main = (ATT_HEADS + 2 * ATT_KV_HEADS) * ATT_HEAD_DIM + IDX_HEADS * IDX_DIM
    n_rest = w_in.shape[1] - n_main
    w_rest = jnp.pad(w_in[:, n_main:], ((0, 0), (0, LANES - n_rest)))
    qkv = _matmul(h, w_in, norm_g=mix_norm, out_dtype=F32, n_cols=n_main, name="dsa_in")
    kw = _matmul(h, w_rest, norm_g=mix_norm, out_dtype=F32, name="dsa_in_idx")
    q, k, v, qi, ki, w = _dsa_rope(qkv, kw)
    o = _dsa_attend(q, k, v, qi, ki, w)
    return _matmul(o, w_out, residual=h, out_dtype=F32, name="dsa_out")


def _mlp_kernel(x_ref, g_ref, wu_ref, wd_ref, o_ref, xn_ref):
    f = pl.program_id(1)

    @pl.when(f == 0)
    def _():
        xf = x_ref[...]
        ms = jnp.mean(xf * xf, axis=-1, keepdims=True)
        xn_ref[...] = (xf * lax.rsqrt(ms + RMS_EPS) * g_ref[...]).astype(BF16)
        o_ref[...] = xf

    u = jnp.maximum(_dot(xn_ref[...], wu_ref[...].astype(BF16)), 0.0)
    o_ref[...] += _dot((u * u).astype(BF16), wd_ref[...].astype(BF16))


def _mlp(h, norm_g, w_up, w_down, tm=1024, tf=512):
    m, d = h.shape
    dff = w_up.shape[1]
    tm = min(tm, m)
    tf = min(tf, dff)
    return pl.pallas_call(
        _mlp_kernel,
        grid=(m // tm, dff // tf),
        in_specs=[
            pl.BlockSpec((tm, d), lambda i, f: (i, 0), pipeline_mode=pl.Buffered(1)),
            pl.BlockSpec((1, d), lambda i, f: (0, 0)),
            pl.BlockSpec((d, tf), lambda i, f: (0, f)),
            pl.BlockSpec((tf, d), lambda i, f: (f, 0)),
        ],
        out_specs=pl.BlockSpec((tm, d), lambda i, f: (i, 0)),
        out_shape=jax.ShapeDtypeStruct((m, d), F32),
        scratch_shapes=[pltpu.VMEM((tm, d), BF16)],
        compiler_params=_cparams(2),
        name="mlp",
    )(h, norm_g.reshape(1, d).astype(F32), w_up, w_down)


def kernel(x, l0_mix_norm, l0_rg_in_w, l0_rg_conv_w, l0_rg_conv_b, l0_rg_wa, l0_rg_ba, l0_rg_wx, l0_rg_bx, l0_rg_lambda, l0_rg_out_w, l0_mlp_norm, l0_mlp_up, l0_mlp_down, l1_mix_norm, l1_dsa_in_w, l1_dsa_out_w, l1_mlp_norm, l1_mlp_up, l1_mlp_down, l2_mix_norm, l2_ssd_in_w, l2_ssd_conv_w, l2_ssd_conv_b, l2_ssd_dt_bias, l2_ssd_a_log, l2_ssd_d, l2_ssd_norm, l2_ssd_out_w, l2_mlp_norm, l2_mlp_up, l2_mlp_down, l3_mix_norm, l3_rg_in_w, l3_rg_conv_w, l3_rg_conv_b, l3_rg_wa, l3_rg_ba, l3_rg_wx, l3_rg_bx, l3_rg_lambda, l3_rg_out_w, l3_mlp_norm, l3_mlp_up, l3_mlp_down, final_norm):
    bsz, seq, d = x.shape
    outs = []
    for b in range(bsz):
        h = x[b]
        h = _rglru_layer(h, l0_mix_norm, l0_rg_in_w, l0_rg_conv_w, l0_rg_conv_b, l0_rg_wa,
                         l0_rg_ba, l0_rg_wx, l0_rg_bx, l0_rg_lambda, l0_rg_out_w)
        h = _mlp(h, l0_mlp_norm, l0_mlp_up, l0_mlp_down)
        h = _dsa_layer(h, l1_mix_norm, l1_dsa_in_w, l1_dsa_out_w)
        h = _mlp(h, l1_mlp_norm, l1_mlp_up, l1_mlp_down)
        h = _ssd_layer(h, l2_mix_norm, l2_ssd_in_w, l2_ssd_conv_w, l2_ssd_conv_b, l2_ssd_dt_bias,
                       l2_ssd_a_log, l2_ssd_d, l2_ssd_norm, l2_ssd_out_w)
        h = _mlp(h, l2_mlp_norm, l2_mlp_up, l2_mlp_down)
        h = _rglru_layer(h, l3_mix_norm, l3_rg_in_w, l3_rg_conv_w, l3_rg_conv_b, l3_rg_wa,
                         l3_rg_ba, l3_rg_wx, l3_rg_bx, l3_rg_lambda, l3_rg_out_w)
        h = _mlp(h, l3_mlp_norm, l3_mlp_up, l3_mlp_down)
        outs.append(_rmsnorm(h, final_norm))
    return jnp.stack(outs, axis=0)
```

```python
import functools
import math

import jax
import jax.numpy as jnp
from jax import lax
from jax.experimental import pallas as pl
from jax.experimental.pallas import tpu as pltpu

F32 = jnp.float32
BF16 = jnp.bfloat16

RMS_EPS = 1e-6
ROPE_THETA = 10000.0
CHUNK = 64
RG_BLOCKS = 8
RG_CONV = 4
RG_C = 8.0
ATT_HEADS = 16
ATT_KV_HEADS = 4
ATT_HEAD_DIM = 128
ATT_GROUP = ATT_HEADS // ATT_KV_HEADS
IDX_HEADS = 16
IDX_DIM = 64
TOPK_MAX = 256
SSD_HEAD_DIM = 64
SSD_GROUPS = 8
SSD_STATE = 128
SSD_CONV = 4

V7X_VMEM_BYTES = 64 * 1024 * 1024
VMEM_LIMIT_BYTES = V7X_VMEM_BYTES - 8 * 1024 * 1024
LANES = 128
SUBLANES = 8

NEG_INF = float("-inf")
KEY_NEG_INF = -2139095041
NEG_INF_BITS = -8388608


def _cparams(n_axes):
    return pltpu.CompilerParams(
        dimension_semantics=("arbitrary",) * n_axes,
        vmem_limit_bytes=VMEM_LIMIT_BYTES,
    )


def _dot(a, b):
    return jnp.dot(a, b, preferred_element_type=F32)


def _dot_nt(a, b):
    return lax.dot_general(a, b, (((1,), (1,)), ((), ())), preferred_element_type=F32)


def _dot_tn(a, b):
    return lax.dot_general(a, b, (((0,), (0,)), ((), ())), preferred_element_type=F32)


def _softplus(x):
    return jnp.maximum(x, 0.0) + jnp.log1p(jnp.exp(-jnp.abs(x)))


def _sigmoid(x):
    return jax.nn.sigmoid(x)


def _mm_kernel(*refs, has_norm, act, has_res, nk, w_rows_are_outputs):
    it = iter(refs)
    x_ref = next(it)
    g_ref = next(it) if has_norm else None
    w_ref = next(it)
    r_ref = next(it) if has_res else None
    o_ref = next(it)
    acc_ref = next(it) if nk > 1 else None
    xn_ref = next(it) if has_norm else None
    j = pl.program_id(1)
    k = pl.program_id(2)

    if has_norm:
        @pl.when(j == 0)
        def _():
            xf = x_ref[...].astype(F32)
            ms = jnp.mean(xf * xf, axis=-1, keepdims=True)
            xn_ref[...] = (xf * lax.rsqrt(ms + RMS_EPS) * g_ref[...]).astype(BF16)

        lhs = xn_ref[...]
    else:
        lhs = x_ref[...].astype(BF16)

    wt = w_ref[...].astype(BF16)
    p = _dot_nt(lhs, wt) if w_rows_are_outputs else _dot(lhs, wt)

    def finish(v):
        if act == "relu2":
            v = jnp.maximum(v, 0.0)
            v = v * v
        if has_res:
            v = v + r_ref[...]
        o_ref[...] = v.astype(o_ref.dtype)

    if nk == 1:
        finish(p)
    else:
        @pl.when(k == 0)
        def _():
            acc_ref[...] = p

        @pl.when(jnp.logical_and(k > 0, k < nk - 1))
        def _():
            acc_ref[...] += p

        @pl.when(k == nk - 1)
        def _():
            finish(acc_ref[...] + p)


def _matmul(x, w, *, norm_g=None, act=None, residual=None, out_dtype=F32,
            tm=2048, tn=512, tk=None, n_cols=None, w_rows_are_outputs=False,
            name="matmul"):
    m, kdim = x.shape
    n_all = w.shape[0] if w_rows_are_outputs else w.shape[1]
    n = n_all if n_cols is None else n_cols
    tm = min(tm, m)
    tn = min(tn, n)
    tk = kdim if tk is None else min(tk, kdim)
    has_norm = norm_g is not None
    if has_norm:
        assert tk == kdim
    assert m % tm == 0 and n % tn == 0 and kdim % tk == 0
    nk = kdim // tk
    grid = (m // tm, n // tn, nk)

    x_mode = {"pipeline_mode": pl.Buffered(1)} if x.dtype == F32 and nk == 1 else {}
    in_specs = [pl.BlockSpec((tm, tk), lambda i, j, k: (i, k), **x_mode)]
    args = [x]
    if has_norm:
        in_specs.append(pl.BlockSpec((1, tk), lambda i, j, k: (0, 0)))
        args.append(norm_g.reshape(1, kdim).astype(F32))
    if w_rows_are_outputs:
        in_specs.append(pl.BlockSpec((tn, tk), lambda i, j, k: (j, k)))
    else:
        in_specs.append(pl.BlockSpec((tk, tn), lambda i, j, k: (k, j)))
    args.append(w)
    if residual is not None:
        in_specs.append(pl.BlockSpec((tm, tn), lambda i, j, k: (i, j)))
        args.append(residual)
    scratch = []
    if nk > 1:
        scratch.append(pltpu.VMEM((tm, tn), F32))
    if has_norm:
        scratch.append(pltpu.VMEM((tm, tk), BF16))

    return pl.pallas_call(
        functools.partial(_mm_kernel, has_norm=has_norm, act=act,
                          has_res=residual is not None, nk=nk,
                          w_rows_are_outputs=w_rows_are_outputs),
        grid=grid,
        in_specs=in_specs,
        out_specs=pl.BlockSpec((tm, tn), lambda i, j, k: (i, j)),
        out_shape=jax.ShapeDtypeStruct((m, n), out_dtype),
        scratch_shapes=scratch,
        compiler_params=_cparams(3),
        name=name,
    )(*args)


def _rmsnorm_kernel(x_ref, g_ref, o_ref):
    xf = x_ref[...]
    ms = jnp.mean(xf * xf, axis=-1, keepdims=True)
    o_ref[...] = xf * lax.rsqrt(ms + RMS_EPS) * g_ref[...]


def _rmsnorm(x, g, tm=512):
    m, d = x.shape
    tm = min(tm, m)
    return pl.pallas_call(
        _rmsnorm_kernel,
        grid=(m // tm,),
        in_specs=[pl.BlockSpec((tm, d), lambda i: (i, 0)),
                  pl.BlockSpec((1, d), lambda i: (0, 0))],
        out_specs=pl.BlockSpec((tm, d), lambda i: (i, 0)),
        out_shape=jax.ShapeDtypeStruct((m, d), F32),
        compiler_params=_cparams(1),
        name="final_norm",
    )(x, g.reshape(1, d))


HALO = SUBLANES


def _causal_conv(buf, x_ref, cw_ref, cb_ref, n_taps):
    t_rows = x_ref.shape[0]
    buf[HALO:HALO + t_rows, :] = x_ref[...]
    out = cb_ref[...]
    for k in range(n_taps):
        off = HALO - (n_taps - 1) + k
        out = out + cw_ref[k:k + 1, :] * buf[off:off + t_rows, :]
    buf[0:HALO, :] = buf[t_rows:t_rows + HALO, :]
    return out


def _rg_kernel(gate_ref, x_ref, cw_ref, cb_ref, wa_ref, wx_ref, ba_ref, bx_ref,
               lam_ref, y_ref, xbuf, a_scr, b_scr, hcar):
    t = pl.program_id(0)
    t_rows, width = x_ref.shape
    n_blocks, bw, _ = wa_ref.shape

    @pl.when(t == 0)
    def _():
        xbuf[0:HALO, :] = jnp.zeros((HALO, width), F32)
        hcar[...] = jnp.zeros_like(hcar)

    xc = _causal_conv(xbuf, x_ref, cw_ref, cb_ref, RG_CONV)
    xcb = xc.astype(BF16)
    sp = _softplus(-lam_ref[...])
    for c in range(n_blocks):
        sl = slice(c * bw, (c + 1) * bw)
        xs = xcb[:, sl]
        r = _sigmoid(_dot(xs, wa_ref[c]) + ba_ref[:, sl])
        i = _sigmoid(_dot(xs, wx_ref[c]) + bx_ref[:, sl])
        log_a = -RG_C * r * sp[:, sl]
        th = jnp.tanh(log_a)
        mult = jnp.sqrt(-2.0 * th / (1.0 - th))
        a_scr[:, sl] = jnp.exp(log_a)
        b_scr[:, sl] = mult * (i * xc[:, sl])

    def step(r, h):
        h = a_scr[pl.ds(r, 1), :] * h + b_scr[pl.ds(r, 1), :]
        b_scr[pl.ds(r, 1), :] = h
        return h

    hcar[...] = lax.fori_loop(0, t_rows, step, hcar[...], unroll=8)
    y_ref[...] = (b_scr[...] * jax.nn.gelu(gate_ref[...])).astype(y_ref.dtype)


def _rg_core(xg, conv_w, conv_b, wa, ba, wx, bx, lam, t_rows=256):
    s, two_w = xg.shape
    width = two_w // 2
    t_rows = min(t_rows, s)
    row = lambda v: v.reshape(1, width).astype(F32)
    full2 = lambda shape: pl.BlockSpec(shape, lambda t: (0, 0))
    full3 = lambda shape: pl.BlockSpec(shape, lambda t: (0, 0, 0))
    return pl.pallas_call(
        _rg_kernel,
        grid=(s // t_rows,),
        in_specs=[
            pl.BlockSpec((t_rows, width), lambda t: (t, 0)),
            pl.BlockSpec((t_rows, width), lambda t: (t, 1)),
            full2((RG_CONV, width)), full2((1, width)),
            full3(wa.shape), full3(wx.shape),
            full2((1, width)), full2((1, width)), full2((1, width)),
        ],
        out_specs=pl.BlockSpec((t_rows, width), lambda t: (t, 0)),
        out_shape=jax.ShapeDtypeStruct((s, width), BF16),
        scratch_shapes=[
            pltpu.VMEM((t_rows + HALO, width), F32),
            pltpu.VMEM((t_rows, width), F32),
            pltpu.VMEM((t_rows, width), F32),
            pltpu.VMEM((1, width), F32),
        ],
        compiler_params=_cparams(1),
        name="rg_core",
    )(xg, xg, conv_w.astype(F32), row(conv_b), wa.astype(BF16), wx.astype(BF16),
      row(ba), row(bx), row(lam))


def _rglru_layer(h, mix_norm, w_in, conv_w, conv_b, w_a, b_a, w_x, b_x, lam, w_out):
    xg = _matmul(h, w_in, norm_g=mix_norm, out_dtype=F32, name="rg_in")
    y = _rg_core(xg, conv_w, conv_b, w_a, b_a, w_x, b_x, lam)
    return _matmul(y, w_out, residual=h, out_dtype=F32, name="rg_out")


SSD_TILE = 128
SSD_HPG = 8
SSD_GW = SSD_HPG * SSD_HEAD_DIM
SSD_GPS = 2


def _cumsum(v, axis):
    n = v.shape[axis]
    idx = lax.broadcasted_iota(jnp.int32, v.shape, axis)
    shift = 1
    while shift < n:
        v = v + jnp.where(idx >= shift, pltpu.roll(v, shift, axis), 0.0)
        shift *= 2
    return v


def _expand_heads(cols, base):
    rows = cols.shape[0]
    lane = lax.broadcasted_iota(jnp.int32, (rows, LANES), 1)
    parts = []
    for v in range(SSD_GW // LANES):
        lo = jnp.broadcast_to(cols[:, base + 2 * v:base + 2 * v + 1], (rows, LANES))
        hi = jnp.broadcast_to(cols[:, base + 2 * v + 1:base + 2 * v + 2], (rows, LANES))
        parts.append(jnp.where(lane < SSD_HEAD_DIM, lo, hi))
    return jnp.concatenate(parts, axis=1)


def _ssd_kernel(z_ref, x_ref, b_ref, c_ref, cwx_ref, cwb_ref, cwc_ref, cbx_ref,
                cbb_ref, cbc_ref, dtr_ref, dbr_ref, alr_ref, dsk_ref, ng_ref,
                y_ref, xbuf, bbuf, cbuf, st):
    t = pl.program_id(1)
    rows = x_ref.shape[0]
    hpg, gw, ns = SSD_HPG, SSD_GW, SSD_STATE

    @pl.when(t == 0)
    def _():
        xbuf[0:HALO, :] = jnp.zeros((HALO, xbuf.shape[1]), F32)
        bbuf[0:HALO, :] = jnp.zeros((HALO, bbuf.shape[1]), F32)
        cbuf[0:HALO, :] = jnp.zeros((HALO, cbuf.shape[1]), F32)
        st[...] = jnp.zeros_like(st)

    def conv_silu(buf, ref, cw, cb):
        v = _causal_conv(buf, ref, cw, cb, SSD_CONV)
        return v * _sigmoid(v)

    xs_all = conv_silu(xbuf, x_ref, cwx_ref, cbx_ref)
    bm_all = conv_silu(bbuf, b_ref, cwb_ref, cbb_ref)
    cm_all = conv_silu(cbuf, c_ref, cwc_ref, cbc_ref)

    li = lax.broadcasted_iota(jnp.int32, (rows, rows), 0)
    si = lax.broadcasted_iota(jnp.int32, (rows, rows), 1)
    causal = li >= si
    eye = jnp.where(li == si, 1.0, 0.0)
    lane = lax.broadcasted_iota(jnp.int32, (rows, LANES), 1)

    for gi in range(x_ref.shape[1] // gw):
        gsl = slice(gi * gw, (gi + 1) * gw)
        xs = xs_all[:, gsl]
        bmb = bm_all[:, gi * ns:(gi + 1) * ns].astype(BF16)
        cmb = cm_all[:, gi * ns:(gi + 1) * ns].astype(BF16)
        hs = slice(gi * hpg, (gi + 1) * hpg)

        dtr = _softplus(dtr_ref[hs, :] + dbr_ref[hs, :])
        csr = _cumsum(dtr * (-jnp.exp(alr_ref[hs, :])), 1)
        both = lax.dot_general(eye, jnp.concatenate([dtr, csr], axis=0),
                               (((1,), (1,)), ((), ())),
                               precision=lax.Precision.HIGHEST,
                               preferred_element_type=F32)
        last = both[rows - 1:rows, :]

        cb = _dot_nt(cmb, bmb)
        xdt = xs * _expand_heads(both, 0)
        xdtb = xdt.astype(BF16)
        ys = []
        for v in range(gw // LANES):
            xv = xdtb[:, v * LANES:(v + 1) * LANES]
            pair = []
            for j in (2 * v, 2 * v + 1):
                seg = both[:, hpg + j:hpg + j + 1] - csr[j:j + 1, :]
                dec = jnp.exp(jnp.where(causal, seg, NEG_INF))
                pair.append(_dot((cb * dec).astype(BF16), xv))
            ys.append(jnp.where(lane < SSD_HEAD_DIM, pair[0], pair[1]))
        y = jnp.concatenate(ys, axis=1)
        stg = st[:, gsl]
        y = y + _dot(cmb, stg.astype(BF16)) * _expand_heads(jnp.exp(both), hpg)
        y = y + xs * dsk_ref[:, gsl]

        z = z_ref[:, gsl]
        yz = y * (z * _sigmoid(z))
        yz = yz * lax.rsqrt(jnp.mean(yz * yz, axis=-1, keepdims=True) + RMS_EPS)
        y_ref[:, gsl] = (yz * ng_ref[:, gsl]).astype(y_ref.dtype)

        xw = (xdt * _expand_heads(jnp.exp(last - both), hpg)).astype(BF16)
        st[:, gsl] = stg * _expand_heads(jnp.exp(last), hpg) + _dot_tn(bmb, xw)


def _ssd_core(zx, dt_raw, conv_w, conv_b, dt_bias, a_log, d_skip, norm_g):
    s = zx.shape[0]
    g = SSD_GROUPS
    gps = SSD_GPS
    inner = g * SSD_GW
    rows = min(SSD_TILE, s)
    xw, bw = gps * SSD_GW, gps * SSD_STATE
    nx = inner // xw
    nb = 2 * inner // bw
    nc = nb + g * SSD_STATE // bw
    cwb0 = inner // bw
    cwc0 = cwb0 + g * SSD_STATE // bw
    hps = gps * SSD_HPG

    dt_t = jnp.transpose(dt_raw)
    colv = lambda v: jnp.pad(v.astype(F32), (0, LANES - v.shape[0])).reshape(LANES, 1)
    conv_w = conv_w.astype(F32)
    conv_b = conv_b.reshape(1, -1).astype(F32)
    dsk = jnp.repeat(d_skip.astype(F32), SSD_HEAD_DIM).reshape(1, inner)
    ng = norm_g.reshape(1, inner).astype(F32)

    gs = lambda shape, fn: pl.BlockSpec(shape, fn)
    in_specs = [
        gs((rows, xw), lambda gi, t: (t, gi)),
        gs((rows, xw), lambda gi, t: (t, nx + gi)),
        gs((rows, bw), lambda gi, t: (t, nb + gi)),
        gs((rows, bw), lambda gi, t: (t, nc + gi)),
        gs((SSD_CONV, xw), lambda gi, t: (0, gi)),
        gs((SSD_CONV, bw), lambda gi, t: (0, cwb0 + gi)),
        gs((SSD_CONV, bw), lambda gi, t: (0, cwc0 + gi)),
        gs((1, xw), lambda gi, t: (0, gi)),
        gs((1, bw), lambda gi, t: (0, cwb0 + gi)),
        gs((1, bw), lambda gi, t: (0, cwc0 + gi)),
        gs((hps, rows), lambda gi, t: (gi, t)),
        gs((hps, 1), lambda gi, t: (gi, 0)),
        gs((hps, 1), lambda gi, t: (gi, 0)),
        gs((1, xw), lambda gi, t: (0, gi)),
        gs((1, xw), lambda gi, t: (0, gi)),
    ]
    return pl.pallas_call(
        _ssd_kernel,
        grid=(g // gps, s // rows),
        in_specs=in_specs,
        out_specs=pl.BlockSpec((rows, xw), lambda gi, t: (t, gi)),
        out_shape=jax.ShapeDtypeStruct((s, inner), BF16),
        scratch_shapes=[
            pltpu.VMEM((rows + HALO, xw), F32),
            pltpu.VMEM((rows + HALO, bw), F32),
            pltpu.VMEM((rows + HALO, bw), F32),
            pltpu.VMEM((SSD_STATE, xw), F32),
        ],
        compiler_params=_cparams(2),
        name="ssd_core",
    )(zx, zx, zx, zx, conv_w, conv_w, conv_w, conv_b, conv_b, conv_b,
      dt_t, colv(dt_bias), colv(a_log), dsk, ng)


def _ssd_layer(h, mix_norm, w_in, conv_w, conv_b, dt_bias, a_log, d_skip, norm_g, w_out):
    heads = dt_bias.shape[0]
    n_zx = w_in.shape[1] - heads
    w_t = jnp.transpose(w_in)
    w_dt = jnp.pad(w_t[n_zx:], ((0, LANES - heads), (0, 0)))
    zx = _matmul(h, w_t, norm_g=mix_norm, out_dtype=F32, n_cols=n_zx,
                 w_rows_are_outputs=True, name="ssd_in")
    dt_raw = _matmul(h, w_dt, norm_g=mix_norm, out_dtype=F32,
                     w_rows_are_outputs=True, name="ssd_in_dt")
    y = _ssd_core(zx, dt_raw, conv_w, conv_b, dt_bias, a_log, d_skip, norm_g)
    return _matmul(y, w_out, residual=h, out_dtype=F32, tm=1024, name="ssd_out")


def _rope_kernel(qkv_ref, kw_ref, cos_ref, sin_ref, cosi_ref, sini_ref,
                 q_ref, k_ref, v_ref, qi_ref, ki_ref, w_ref, *, q_scale, w_scale):
    rows = qkv_ref.shape[0]
    lane = lax.broadcasted_iota(jnp.int32, (rows, LANES), 1)
    cos, sin = cos_ref[...], sin_ref[...]
    cosi, sini = cosi_ref[...], sini_ref[...]
    half_i = IDX_DIM // 2
    lower_i = (lane & (IDX_DIM - 1)) < half_i

    def rope_head(x):
        return x * cos + pltpu.roll(x, ATT_HEAD_DIM // 2, 1) * sin

    def rope_idx(x):
        rot = jnp.where(lower_i, pltpu.roll(x, LANES - half_i, 1), pltpu.roll(x, half_i, 1))
        return x * cosi + rot * sini

    nq = q_ref.shape[1] // LANES
    nkv = k_ref.shape[1] // LANES
    nqi = qi_ref.shape[1] // LANES
    for hh in range(nq):
        x = qkv_ref[:, hh * LANES:(hh + 1) * LANES]
        q_ref[:, hh * LANES:(hh + 1) * LANES] = (rope_head(x) * q_scale).astype(q_ref.dtype)
    off = nq
    for hh in range(nkv):
        x = qkv_ref[:, (off + hh) * LANES:(off + hh + 1) * LANES]
        k_ref[:, hh * LANES:(hh + 1) * LANES] = rope_head(x).astype(k_ref.dtype)
    off += nkv
    v_ref[...] = qkv_ref[:, off * LANES:(off + nkv) * LANES].astype(v_ref.dtype)
    off += nkv
    for hh in range(nqi):
        x = qkv_ref[:, (off + hh) * LANES:(off + hh + 1) * LANES]
        qi_ref[:, hh * LANES:(hh + 1) * LANES] = rope_idx(x).astype(qi_ref.dtype)
    kw = kw_ref[...]
    swapped = pltpu.roll(kw, LANES // 2, 1)
    ki_ref[...] = rope_idx(jnp.where(lane < IDX_DIM, kw, swapped)).astype(ki_ref.dtype)
    w_ref[...] = jnp.where(lane < IDX_HEADS, swapped * w_scale, 0.0)


def _rope_tables(seq):
    def table(dim):
        inv = ROPE_THETA ** (-jnp.arange(0, dim, 2, dtype=F32) / dim)
        ang = jnp.arange(seq, dtype=F32)[:, None] * inv[None, :]
        return jnp.cos(ang), jnp.sin(ang)

    c, s = table(ATT_HEAD_DIM)
    ci, si = table(IDX_DIM)
    return (jnp.concatenate([c, c], axis=1), jnp.concatenate([-s, s], axis=1),
            jnp.concatenate([ci, ci, ci, ci], axis=1), jnp.concatenate([-si, si, -si, si], axis=1))


def _dsa_rope(qkv, kw, t_rows=256):
    s = qkv.shape[0]
    t_rows = min(t_rows, s)
    dq = ATT_HEADS * ATT_HEAD_DIM
    dkv = ATT_KV_HEADS * ATT_HEAD_DIM
    dqi = IDX_HEADS * IDX_DIM
    cos, sin, cosi, sini = _rope_tables(s)
    tile = lambda width: pl.BlockSpec((t_rows, width), lambda t: (t, 0))
    return pl.pallas_call(
        functools.partial(_rope_kernel, q_scale=ATT_HEAD_DIM ** -0.5 * math.log2(math.e),
                          w_scale=IDX_HEADS ** -0.5 * IDX_DIM ** -0.5),
        grid=(s // t_rows,),
        in_specs=[tile(qkv.shape[1]), tile(LANES), tile(LANES), tile(LANES), tile(LANES), tile(LANES)],
        out_specs=[tile(dq), tile(dkv), tile(dkv), tile(dqi), tile(LANES), tile(LANES)],
        out_shape=[
            jax.ShapeDtypeStruct((s, dq), BF16),
            jax.ShapeDtypeStruct((s, dkv), BF16),
            jax.ShapeDtypeStruct((s, dkv), BF16),
            jax.ShapeDtypeStruct((s, dqi), BF16),
            jax.ShapeDtypeStruct((s, LANES), BF16),
            jax.ShapeDtypeStruct((s, LANES), F32),
        ],
        compiler_params=_cparams(1),
        name="dsa_rope",
    )(qkv, kw, cos, sin, cosi, sini)


DSA_TQ = 128
DSA_KB = 512


def _sortable_key(score):
    bits = lax.bitcast_convert_type(score, jnp.int32)
    return jnp.where(bits < 0, bits ^ jnp.int32(0x7FFFFFFF), bits)


def _key_hi(key):
    return jnp.right_shift(key, 16).astype(jnp.int16)


def _key_lo(key):
    return ((key & 0xFFFF) - 32768).astype(jnp.int16)


def _tile_lanes(v, width):
    return jnp.concatenate([v] * (width // v.shape[1]), axis=1)


def _dsa_kernel(q_ref, qi_ref, w_ref, k_ref, v_ref, ki_ref, o_ref,
                sc, hi16, lo16, qa, wb, qs, m_scr, acc_scr, *, top_k):
    i = pl.program_id(0)
    tq = q_ref.shape[0]
    kb_size = sc.shape[2]
    nkb = ((i + 1) * tq + kb_size - 1) // kb_size
    lane = lax.broadcasted_iota(jnp.int32, (tq, LANES), 1)
    chunk_shift = CHUNK.bit_length() - 1
    q_chunk = (i * tq + lax.broadcasted_iota(jnp.int32, (tq, kb_size), 0)) >> chunk_shift
    k_lane = lax.broadcasted_iota(jnp.int32, (tq, kb_size), 1)

    def admissible(kb):
        return ((kb * kb_size + k_lane) >> chunk_shift) <= q_chunk

    for h in range(IDX_HEADS):
        pair = qi_ref[:, (h // 2) * LANES:(h // 2 + 1) * LANES]
        keep = (lane < IDX_DIM) if h % 2 == 0 else (lane >= IDX_DIM)
        qa[h] = jnp.where(keep, pair, jnp.zeros_like(pair))
        wb[h] = jnp.broadcast_to(w_ref[:, h:h + 1], (tq, LANES))

    def score_block(kb, carry):
        kis = ki_ref[pl.ds(pl.multiple_of(kb * kb_size, kb_size), kb_size), :]
        score = jnp.zeros((tq, kb_size), F32)
        for h in range(IDX_HEADS):
            lg = _dot_nt(qa[h], kis)
            score = score + jnp.maximum(lg, 0.0) * _tile_lanes(wb[h], kb_size)
        key = jnp.where(admissible(kb), _sortable_key(score), jnp.int32(KEY_NEG_INF))
        sc[kb] = key
        hi16[kb] = _key_hi(key)
        lo16[kb] = _key_lo(key)
        return carry

    lax.fori_loop(0, nkb, score_block, 0)

    @pl.when(nkb % 2 == 1)
    def _():
        pad = jnp.full((tq, kb_size), KEY_NEG_INF, jnp.int32)
        hi16[nkb] = _key_hi(pad)
        lo16[nkb] = _key_lo(pad)

    n_pairs = (nkb + 1) // 2
    i16 = jnp.int16

    def count16(arr, cand, strict):
        cand_t = _tile_lanes(cand.astype(i16), kb_size)

        def count_pair(pi, acc):
            for u in range(2):
                blk = arr[2 * pi + u]
                hit = (blk > cand_t) if strict else (blk >= cand_t)
                one = jnp.where(hit, jnp.ones_like(blk), jnp.zeros_like(blk))
                for c in range(kb_size // LANES):
                    acc = acc + one[:, c * LANES:(c + 1) * LANES]
            return acc

        acc = lax.fori_loop(0, n_pairs, count_pair, jnp.zeros((tq, LANES), i16))
        cnt = jnp.sum(acc.astype(jnp.int32).astype(F32), axis=1, keepdims=True)
        return jnp.broadcast_to(cnt, (tq, LANES))

    def kth_largest16(arr, target):
        def bit_step(b, thr):
            cand = thr + jnp.left_shift(jnp.int32(1), 15 - b)
            return jnp.where(count16(arr, cand, False) >= target, cand, thr)

        return lax.fori_loop(0, 16, bit_step, jnp.full((tq, LANES), -32768, jnp.int32))

    k_f = jnp.full((tq, LANES), float(top_k), F32)
    thr_hi = kth_largest16(hi16, k_f)
    need = k_f - count16(hi16, thr_hi, True)
    thr_hi_t = _tile_lanes(thr_hi.astype(i16), kb_size)

    def tie_block(kb, carry):
        lo16[kb] = jnp.where(hi16[kb] == thr_hi_t, lo16[kb], jnp.full((tq, kb_size), -32768, i16))
        return carry

    lax.fori_loop(0, 2 * n_pairs, tie_block, 0)
    thr_lo = kth_largest16(lo16, need)
    thr = jnp.left_shift(thr_hi, 16) | ((thr_lo + 32768) & 0xFFFF)
    thr_t = _tile_lanes(thr, kb_size)

    n_above = count16(lo16, thr_lo, True)
    need_eq = need - n_above
    n_eq = count16(lo16, thr_lo, False) - n_above
    real_thr = thr > KEY_NEG_INF
    extra = jnp.where(jnp.logical_and(real_thr, n_eq > need_eq), 1.0, 0.0)
    has_extra_ties = jnp.max(extra) > 0.0

    @pl.when(jnp.logical_not(has_extra_ties))
    def _():
        def bias_block(kb, carry):
            sel = jnp.logical_and(sc[kb] >= thr_t, admissible(kb))
            sc[kb] = jnp.where(sel, jnp.int32(0), jnp.int32(NEG_INF_BITS))
            return carry

        lax.fori_loop(0, nkb, bias_block, 0)

    @pl.when(has_extra_ties)
    def _():
        ki_ = lax.broadcasted_iota(jnp.int32, (kb_size, kb_size), 0)
        kj_ = lax.broadcasted_iota(jnp.int32, (kb_size, kb_size), 1)
        before = jnp.where(ki_ < kj_, 1.0, 0.0).astype(BF16)
        limit = _tile_lanes(jnp.where(real_thr, need_eq, float(kb_size * sc.shape[0])), kb_size)

        def bias_block(kb, seen):
            key = sc[kb]
            adm = admissible(kb)
            eq = jnp.logical_and(key == thr_t, adm)
            eq_b = jnp.where(eq, 1.0, 0.0).astype(BF16)
            rank = _dot(eq_b, before) + _tile_lanes(seen, kb_size)
            keep = jnp.logical_or(jnp.logical_and(key > thr_t, adm),
                                  jnp.logical_and(eq, rank < limit))
            sc[kb] = jnp.where(keep, jnp.int32(0), jnp.int32(NEG_INF_BITS))
            total = _dot(eq_b, jnp.ones((kb_size, LANES), BF16))
            return seen + total

        lax.fori_loop(0, nkb, bias_block, jnp.zeros((tq, LANES), F32))

    for g in range(ATT_KV_HEADS):
        qs[g] = jnp.concatenate(
            [q_ref[:, (g * ATT_GROUP + a) * LANES:(g * ATT_GROUP + a + 1) * LANES]
             for a in range(ATT_GROUP)], axis=0)
    m_scr[...] = jnp.full(m_scr.shape, NEG_INF, F32)
    acc_scr[...] = jnp.zeros_like(acc_scr)
    ones = jnp.ones((kb_size, LANES), BF16)

    def attn_block(kb):
        start = pl.multiple_of(kb * kb_size, kb_size)
        bias = lax.bitcast_convert_type(sc[kb], F32)
        bias = jnp.concatenate([bias] * ATT_GROUP, axis=0)
        for g in range(ATT_KV_HEADS):
            kblk = k_ref[pl.ds(start, kb_size), g * LANES:(g + 1) * LANES]
            vblk = v_ref[pl.ds(start, kb_size), g * LANES:(g + 1) * LANES]
            s = _dot_nt(qs[g], kblk) + bias
            m_old = m_scr[g]
            m_new = jnp.maximum(m_old, jnp.max(s, axis=1, keepdims=True))
            m_safe = jnp.where(m_new == NEG_INF, 0.0, m_new)
            alpha = jnp.exp2(m_old - m_safe)
            p = jnp.exp2(s - _tile_lanes(m_safe, kb_size))
            pv = _dot(p.astype(BF16), jnp.concatenate([vblk, ones], axis=1))
            acc_scr[g] = _tile_lanes(alpha, 2 * LANES) * acc_scr[g] + pv
            m_scr[g] = m_new

    def attn_pair(pi, carry):
        attn_block(2 * pi)
        attn_block(2 * pi + 1)
        return carry

    lax.fori_loop(0, nkb // 2, attn_pair, 0)

    @pl.when(nkb % 2 == 1)
    def _():
        attn_block(nkb - 1)

    for g in range(ATT_KV_HEADS):
        out = acc_scr[g][:, :LANES] / acc_scr[g][:, LANES:]
        for a in range(ATT_GROUP):
            hh = g * ATT_GROUP + a
            o_ref[:, hh * LANES:(hh + 1) * LANES] = out[a * tq:(a + 1) * tq].astype(o_ref.dtype)


def _dsa_attend(q, k, v, qi, ki, w):
    s = q.shape[0]
    tq = min(DSA_TQ, s)
    kb = min(DSA_KB, s)
    top_k = min(TOPK_MAX, s // 4)
    resident = lambda width: pl.BlockSpec((s, width), lambda i: (0, 0),
                                          pipeline_mode=pl.Buffered(1))
    tile = lambda width: pl.BlockSpec((tq, width), lambda i: (i, 0))
    return pl.pallas_call(
        functools.partial(_dsa_kernel, top_k=top_k),
        grid=(s // tq,),
        in_specs=[tile(q.shape[1]), tile(qi.shape[1]), tile(LANES),
                  resident(k.shape[1]), resident(v.shape[1]), resident(LANES)],
        out_specs=tile(q.shape[1]),
        out_shape=jax.ShapeDtypeStruct(q.shape, BF16),
        scratch_shapes=[
            pltpu.VMEM((s // kb, tq, kb), jnp.int32),
            pltpu.VMEM((2 * pl.cdiv(s // kb, 2), tq, kb), jnp.int16),
            pltpu.VMEM((2 * pl.cdiv(s // kb, 2), tq, kb), jnp.int16),
            pltpu.VMEM((IDX_HEADS, tq, LANES), BF16),
            pltpu.VMEM((IDX_HEADS, tq, LANES), F32),
            pltpu.VMEM((ATT_KV_HEADS, ATT_GROUP * tq, LANES), BF16),
            pltpu.VMEM((ATT_KV_HEADS, ATT_GROUP * tq, LANES), F32),
            pltpu.VMEM((ATT_KV_HEADS, ATT_GROUP * tq, 2 * LANES), F32),
        ],
        compiler_params=_cparams(1),
        name="dsa_attend",
    )(q, qi, w, k, v, ki)


def _dsa_layer(h, mix_norm, w_in, w_out):
    n_main = (ATT_HEADS + 2 * ATT_KV_HEADS) * ATT_HEAD_DIM + IDX_HEADS * IDX_DIM
    n_rest = w_in.shape[1] - n_main
    w_t = jnp.transpose(w_in)
    w_rest = jnp.pad(w_t[n_main:], ((0, LANES - n_rest), (0, 0)))
    qkv = _matmul(h, w_t, norm_g=mix_norm, out_dtype=F32, n_cols=n_main,
                  w_rows_are_outputs=True, name="dsa_in")
    kw = _matmul(h, w_rest, norm_g=mix_norm, out_dtype=F32,
                 w_rows_are_outputs=True, name="dsa_in_idx")
    q, k, v, qi, ki, w = _dsa_rope(qkv, kw)
    o = _dsa_attend(q, k, v, qi, ki, w)
    return _matmul(o, w_out, residual=h, out_dtype=F32, name="dsa_out")


def _mlp_kernel(x_ref, g_ref, wu_ref, wd_ref, o_ref, xn_ref):
    f = pl.program_id(1)

    @pl.when(f == 0)
    def _():
        xf = x_ref[...]
        ms = jnp.mean(xf * xf, axis=-1, keepdims=True)
        xn_ref[...] = (xf * lax.rsqrt(ms + RMS_EPS) * g_ref[...]).astype(BF16)
        o_ref[...] = xf

    u = jnp.maximum(_dot(xn_ref[...], wu_ref[...].astype(BF16)), 0.0)
    o_ref[...] += _dot((u * u).astype(BF16), wd_ref[...].astype(BF16))


def _mlp(h, norm_g, w_up, w_down, tm=1024, tf=512):
    m, d = h.shape
    dff = w_up.shape[1]
    tm = min(tm, m)
    tf = min(tf, dff)
    return pl.pallas_call(
        _mlp_kernel,
        grid=(m // tm, dff // tf),
        in_specs=[
            pl.BlockSpec((tm, d), lambda i, f: (i, 0), pipeline_mode=pl.Buffered(1)),
            pl.BlockSpec((1, d), lambda i, f: (0, 0)),
            pl.BlockSpec((d, tf), lambda i, f: (0, f)),
            pl.BlockSpec((tf, d), lambda i, f: (f, 0)),
        ],
        out_specs=pl.BlockSpec((tm, d), lambda i, f: (i, 0)),
        out_shape=jax.ShapeDtypeStruct((m, d), F32),
        scratch_shapes=[pltpu.VMEM((tm, d), BF16)],
        compiler_params=_cparams(2),
        name="mlp",
    )(h, norm_g.reshape(1, d).astype(F32), w_up, w_down)


def kernel(x, l0_mix_norm, l0_rg_in_w, l0_rg_conv_w, l0_rg_conv_b, l0_rg_wa, l0_rg_ba, l0_rg_wx, l0_rg_bx, l0_rg_lambda, l0_rg_out_w, l0_mlp_norm, l0_mlp_up, l0_mlp_down, l1_mix_norm, l1_dsa_in_w, l1_dsa_out_w, l1_mlp_norm, l1_mlp_up, l1_mlp_down, l2_mix_norm, l2_ssd_in_w, l2_ssd_conv_w, l2_ssd_conv_b, l2_ssd_dt_bias, l2_ssd_a_log, l2_ssd_d, l2_ssd_norm, l2_ssd_out_w, l2_mlp_norm, l2_mlp_up, l2_mlp_down, l3_mix_norm, l3_rg_in_w, l3_rg_conv_w, l3_rg_conv_b, l3_rg_wa, l3_rg_ba, l3_rg_wx, l3_rg_bx, l3_rg_lambda, l3_rg_out_w, l3_mlp_norm, l3_mlp_up, l3_mlp_down, final_norm):
    bsz, seq, d = x.shape
    outs = []
    for b in range(bsz):
        h = x[b]
        h = _rglru_layer(h, l0_mix_norm, l0_rg_in_w, l0_rg_conv_w, l0_rg_conv_b, l0_rg_wa,
                         l0_rg_ba, l0_rg_wx, l0_rg_bx, l0_rg_lambda, l0_rg_out_w)
        h = _mlp(h, l0_mlp_norm, l0_mlp_up, l0_mlp_down)
        h = _dsa_layer(h, l1_mix_norm, l1_dsa_in_w, l1_dsa_out_w)
        h = _mlp(h, l1_mlp_norm, l1_mlp_up, l1_mlp_down)
        h = _ssd_layer(h, l2_mix_norm, l2_ssd_in_w, l2_ssd_conv_w, l2_ssd_conv_b, l2_ssd_dt_bias,
                       l2_ssd_a_log, l2_ssd_d, l2_ssd_norm, l2_ssd_out_w)
        h = _mlp(h, l2_mlp_norm, l2_mlp_up, l2_mlp_down)
        h = _rglru_layer(h, l3_mix_norm, l3_rg_in_w, l3_rg_conv_w, l3_rg_conv_b, l3_rg_wa,
                         l3_rg_ba, l3_rg_wx, l3_rg_bx, l3_rg_lambda, l3_rg_out_w)
        h = _mlp(h, l3_mlp_norm, l3_mlp_up, l3_mlp_down)
        outs.append(_rmsnorm(h, final_norm))
    return jnp.stack(outs, axis=0)
```

```python
import functools
import math

import jax
import jax.numpy as jnp
from jax import lax
from jax.experimental import pallas as pl
from jax.experimental.pallas import tpu as pltpu

F32 = jnp.float32
BF16 = jnp.bfloat16

RMS_EPS = 1e-6
ROPE_THETA = 10000.0
CHUNK = 64
RG_BLOCKS = 8
RG_CONV = 4
RG_C = 8.0
ATT_HEADS = 16
ATT_KV_HEADS = 4
ATT_HEAD_DIM = 128
ATT_GROUP = ATT_HEADS // ATT_KV_HEADS
IDX_HEADS = 16
IDX_DIM = 64
TOPK_MAX = 256
SSD_HEAD_DIM = 64
SSD_GROUPS = 8
SSD_STATE = 128
SSD_CONV = 4

V7X_VMEM_BYTES = 64 * 1024 * 1024
VMEM_LIMIT_BYTES = V7X_VMEM_BYTES - 8 * 1024 * 1024
LANES = 128
SUBLANES = 8

NEG_INF = float("-inf")
KEY_NEG_INF = -2139095041
NEG_INF_BITS = -8388608


def _cparams(n_axes):
    return pltpu.CompilerParams(
        dimension_semantics=("arbitrary",) * n_axes,
        vmem_limit_bytes=VMEM_LIMIT_BYTES,
    )


def _dot(a, b):
    return jnp.dot(a, b, preferred_element_type=F32)


def _dot_nt(a, b):
    return lax.dot_general(a, b, (((1,), (1,)), ((), ())), preferred_element_type=F32)


def _dot_tn(a, b):
    return lax.dot_general(a, b, (((0,), (0,)), ((), ())), preferred_element_type=F32)


def _softplus(x):
    return jnp.maximum(x, 0.0) + jnp.log1p(jnp.exp(-jnp.abs(x)))


def _sigmoid(x):
    return jax.nn.sigmoid(x)


def _mm_kernel(*refs, has_norm, has_res, has_extra, w_rows_are_outputs):
    it = iter(refs)
    x_ref = next(it)
    g_ref = next(it) if has_norm else None
    w_ref = next(it)
    r_ref = next(it) if has_res else None
    we_ref = next(it) if has_extra else None
    o_ref = next(it)
    oe_ref = next(it) if has_extra else None
    xn_ref = next(it) if has_norm else None
    j = pl.program_id(1)

    if has_norm:
        @pl.when(j == 0)
        def _():
            xf = x_ref[...].astype(F32)
            ms = jnp.mean(xf * xf, axis=-1, keepdims=True)
            xn_ref[...] = (xf * lax.rsqrt(ms + RMS_EPS) * g_ref[...]).astype(BF16)

        lhs_ref = xn_ref
    else:
        lhs_ref = x_ref

    if has_extra:
        @pl.when(j == 0)
        def _():
            oe_ref[...] = _dot_nt(lhs_ref[...].astype(BF16), we_ref[...].astype(BF16))

    wt = w_ref[...].astype(BF16)
    lhs = lhs_ref[...].astype(BF16)
    v = _dot_nt(lhs, wt) if w_rows_are_outputs else _dot(lhs, wt)
    if has_res:
        v = v + r_ref[...]
    o_ref[...] = v.astype(o_ref.dtype)


def _matmul(x, w, *, norm_g=None, residual=None, out_dtype=F32, tm=2048, tn=512,
            n_cols=None, w_rows_are_outputs=False, w_extra=None, name="matmul"):
    m, kdim = x.shape
    n_all = w.shape[0] if w_rows_are_outputs else w.shape[1]
    n = n_all if n_cols is None else n_cols
    tm = min(tm, m)
    tn = min(tn, n)
    has_norm = norm_g is not None
    assert m % tm == 0 and n % tn == 0
    grid = (m // tm, n // tn)

    x_mode = {"pipeline_mode": pl.Buffered(1)} if x.dtype == F32 else {}
    in_specs = [pl.BlockSpec((tm, kdim), lambda i, j: (i, 0), **x_mode)]
    args = [x]
    if has_norm:
        in_specs.append(pl.BlockSpec((1, kdim), lambda i, j: (0, 0)))
        args.append(norm_g.reshape(1, kdim).astype(F32))
    if w_rows_are_outputs:
        in_specs.append(pl.BlockSpec((tn, kdim), lambda i, j: (j, 0)))
    else:
        in_specs.append(pl.BlockSpec((kdim, tn), lambda i, j: (0, j)))
    args.append(w)
    if residual is not None:
        in_specs.append(pl.BlockSpec((tm, tn), lambda i, j: (i, j)))
        args.append(residual)
    out_specs = pl.BlockSpec((tm, tn), lambda i, j: (i, j))
    out_shape = jax.ShapeDtypeStruct((m, n), out_dtype)
    if w_extra is not None:
        in_specs.append(pl.BlockSpec((LANES, kdim), lambda i, j: (0, 0)))
        args.append(w_extra)
        out_specs = [out_specs, pl.BlockSpec((tm, LANES), lambda i, j: (i, 0))]
        out_shape = [out_shape, jax.ShapeDtypeStruct((m, LANES), F32)]

    return pl.pallas_call(
        functools.partial(_mm_kernel, has_norm=has_norm, has_res=residual is not None,
                          has_extra=w_extra is not None,
                          w_rows_are_outputs=w_rows_are_outputs),
        grid=grid,
        in_specs=in_specs,
        out_specs=out_specs,
        out_shape=out_shape,
        scratch_shapes=[pltpu.VMEM((tm, kdim), BF16)] if has_norm else [],
        compiler_params=_cparams(2),
        name=name,
    )(*args)


HALO = SUBLANES


def _causal_conv(buf, x_ref, cw_ref, cb_ref, n_taps):
    t_rows = x_ref.shape[0]
    buf[HALO:HALO + t_rows, :] = x_ref[...]
    ext = buf[...]
    acc = cw_ref[0:1, :] * ext
    for k in range(1, n_taps):
        acc = pltpu.roll(acc, 1, 0) + cw_ref[k:k + 1, :] * ext
    buf[0:HALO, :] = buf[t_rows:t_rows + HALO, :]
    return cb_ref[...] + acc[HALO:, :]


def _rg_kernel(gate_ref, x_ref, cw_ref, cb_ref, wa_ref, wx_ref, ba_ref, bx_ref,
               lam_ref, y_ref, xbuf, a_scr, b_scr, hcar):
    t = pl.program_id(0)
    t_rows, width = x_ref.shape
    n_blocks, bw, _ = wa_ref.shape

    @pl.when(t == 0)
    def _():
        xbuf[0:HALO, :] = jnp.zeros((HALO, width), F32)
        hcar[...] = jnp.zeros_like(hcar)

    xc = _causal_conv(xbuf, x_ref, cw_ref, cb_ref, RG_CONV)
    xcb = xc.astype(BF16)
    sp = _softplus(-lam_ref[...])
    for c in range(n_blocks):
        sl = slice(c * bw, (c + 1) * bw)
        xs = xcb[:, sl]
        r = _sigmoid(_dot(xs, wa_ref[c]) + ba_ref[:, sl])
        i = _sigmoid(_dot(xs, wx_ref[c]) + bx_ref[:, sl])
        log_a = -RG_C * r * sp[:, sl]
        th = jnp.tanh(log_a)
        mult = jnp.sqrt(-2.0 * th / (1.0 - th))
        a_scr[:, sl] = jnp.exp(log_a)
        b_scr[:, sl] = mult * (i * xc[:, sl])

    def step(r, h):
        h = a_scr[pl.ds(r, 1), :] * h + b_scr[pl.ds(r, 1), :]
        b_scr[pl.ds(r, 1), :] = h
        return h

    hcar[...] = lax.fori_loop(0, t_rows, step, hcar[...], unroll=8)
    y_ref[...] = (b_scr[...] * jax.nn.gelu(gate_ref[...])).astype(y_ref.dtype)


def _rg_core(xg, conv_w, conv_b, wa, ba, wx, bx, lam, t_rows=256):
    s, two_w = xg.shape
    width = two_w // 2
    t_rows = min(t_rows, s)
    row = lambda v: v.reshape(1, width).astype(F32)
    full2 = lambda shape: pl.BlockSpec(shape, lambda t: (0, 0))
    full3 = lambda shape: pl.BlockSpec(shape, lambda t: (0, 0, 0))
    return pl.pallas_call(
        _rg_kernel,
        grid=(s // t_rows,),
        in_specs=[
            pl.BlockSpec((t_rows, width), lambda t: (t, 0)),
            pl.BlockSpec((t_rows, width), lambda t: (t, 1)),
            full2((RG_CONV, width)), full2((1, width)),
            full3(wa.shape), full3(wx.shape),
            full2((1, width)), full2((1, width)), full2((1, width)),
        ],
        out_specs=pl.BlockSpec((t_rows, width), lambda t: (t, 0)),
        out_shape=jax.ShapeDtypeStruct((s, width), BF16),
        scratch_shapes=[
            pltpu.VMEM((t_rows + HALO, width), F32),
            pltpu.VMEM((t_rows, width), F32),
            pltpu.VMEM((t_rows, width), F32),
            pltpu.VMEM((1, width), F32),
        ],
        compiler_params=_cparams(1),
        name="rg_core",
    )(xg, xg, conv_w.astype(F32), row(conv_b), wa.astype(BF16), wx.astype(BF16),
      row(ba), row(bx), row(lam))


def _rglru_layer(h, mix_norm, w_in, conv_w, conv_b, w_a, b_a, w_x, b_x, lam, w_out):
    xg = _matmul(h, w_in, norm_g=mix_norm, out_dtype=F32, name="rg_in")
    y = _rg_core(xg, conv_w, conv_b, w_a, b_a, w_x, b_x, lam)
    return _matmul(y, w_out, residual=h, out_dtype=F32, name="rg_out")


SSD_TILE = 128
SSD_HPG = 8
SSD_GW = SSD_HPG * SSD_HEAD_DIM
SSD_GPS = 4


def _cumsum(v, axis):
    n = v.shape[axis]
    idx = lax.broadcasted_iota(jnp.int32, v.shape, axis)
    shift = 1
    while shift < n:
        v = v + jnp.where(idx >= shift, pltpu.roll(v, shift, axis), 0.0)
        shift *= 2
    return v


def _expand_heads(cols, base):
    rows = cols.shape[0]
    lane = lax.broadcasted_iota(jnp.int32, (rows, LANES), 1)
    parts = []
    for v in range(SSD_GW // LANES):
        lo = jnp.broadcast_to(cols[:, base + 2 * v:base + 2 * v + 1], (rows, LANES))
        hi = jnp.broadcast_to(cols[:, base + 2 * v + 1:base + 2 * v + 2], (rows, LANES))
        parts.append(jnp.where(lane < SSD_HEAD_DIM, lo, hi))
    return jnp.concatenate(parts, axis=1)


def _ssd_kernel(z_ref, x_ref, b_ref, c_ref, cwx_ref, cwb_ref, cwc_ref, cbx_ref,
                cbb_ref, cbc_ref, dtr_ref, dbr_ref, alr_ref, dsk_ref, ng_ref,
                y_ref, xbuf, bbuf, cbuf, st):
    t = pl.program_id(1)
    rows = x_ref.shape[0]
    hpg, gw, ns = SSD_HPG, SSD_GW, SSD_STATE

    @pl.when(t == 0)
    def _():
        xbuf[0:HALO, :] = jnp.zeros((HALO, xbuf.shape[1]), F32)
        bbuf[0:HALO, :] = jnp.zeros((HALO, bbuf.shape[1]), F32)
        cbuf[0:HALO, :] = jnp.zeros((HALO, cbuf.shape[1]), F32)
        st[...] = jnp.zeros_like(st)

    def conv_silu(buf, ref, cw, cb):
        v = _causal_conv(buf, ref, cw, cb, SSD_CONV)
        return v * _sigmoid(v)

    xs_all = conv_silu(xbuf, x_ref, cwx_ref, cbx_ref)
    bm_all = conv_silu(bbuf, b_ref, cwb_ref, cbb_ref)
    cm_all = conv_silu(cbuf, c_ref, cwc_ref, cbc_ref)

    li = lax.broadcasted_iota(jnp.int32, (rows, rows), 0)
    si = lax.broadcasted_iota(jnp.int32, (rows, rows), 1)
    causal = li >= si
    eye = jnp.where(li == si, 1.0, 0.0)
    lane = lax.broadcasted_iota(jnp.int32, (rows, LANES), 1)

    groups = range(x_ref.shape[1] // gw)
    gsl = [slice(gi * gw, (gi + 1) * gw) for gi in groups]
    xs = [xs_all[:, gsl[gi]] for gi in groups]
    bmb = [bm_all[:, gi * ns:(gi + 1) * ns].astype(BF16) for gi in groups]
    cmb = [cm_all[:, gi * ns:(gi + 1) * ns].astype(BF16) for gi in groups]
    cb = [_dot_nt(cmb[gi], bmb[gi]) for gi in groups]
    stg = [st[:, gsl[gi]] for gi in groups]
    y_off = [_dot(cmb[gi], stg[gi].astype(BF16)) for gi in groups]

    csr, both = [], []
    for gi in groups:
        hs = slice(gi * hpg, (gi + 1) * hpg)
        dtr = _softplus(dtr_ref[hs, :] + dbr_ref[hs, :])
        csr.append(_cumsum(dtr * (-jnp.exp(alr_ref[hs, :])), 1))
        both.append(lax.dot_general(eye, jnp.concatenate([dtr, csr[gi]], axis=0),
                                    (((1,), (1,)), ((), ())),
                                    precision=lax.Precision.HIGHEST,
                                    preferred_element_type=F32))
    last = [both[gi][rows - 1:rows, :] for gi in groups]

    xdt = [xs[gi] * _expand_heads(both[gi], 0) for gi in groups]
    xdtb = [xdt[gi].astype(BF16) for gi in groups]
    xw = [(xdt[gi] * _expand_heads(jnp.exp(last[gi] - both[gi]), hpg)).astype(BF16)
          for gi in groups]
    st_new = [_dot_tn(bmb[gi], xw[gi]) for gi in groups]

    ys = [[] for _ in groups]
    for v in range(gw // LANES):
        pair = [[] for _ in groups]
        for j in (2 * v, 2 * v + 1):
            for gi in groups:
                seg = both[gi][:, hpg + j:hpg + j + 1] - csr[gi][j:j + 1, :]
                dec = jnp.exp(jnp.where(causal, seg, NEG_INF))
                pair[gi].append(_dot((cb[gi] * dec).astype(BF16),
                                     xdtb[gi][:, v * LANES:(v + 1) * LANES]))
        for gi in groups:
            ys[gi].append(jnp.where(lane < SSD_HEAD_DIM, pair[gi][0], pair[gi][1]))

    for gi in groups:
        y = jnp.concatenate(ys[gi], axis=1)
        y = y + y_off[gi] * _expand_heads(jnp.exp(both[gi]), hpg)
        y = y + xs[gi] * dsk_ref[:, gsl[gi]]
        z = z_ref[:, gsl[gi]]
        yz = y * (z * _sigmoid(z))
        yz = yz * lax.rsqrt(jnp.mean(yz * yz, axis=-1, keepdims=True) + RMS_EPS)
        y_ref[:, gsl[gi]] = (yz * ng_ref[:, gsl[gi]]).astype(y_ref.dtype)
        st[:, gsl[gi]] = stg[gi] * _expand_heads(jnp.exp(last[gi]), hpg) + st_new[gi]


def _ssd_core(zx, dt_raw, conv_w, conv_b, dt_bias, a_log, d_skip, norm_g):
    s = zx.shape[0]
    g = SSD_GROUPS
    gps = SSD_GPS
    inner = g * SSD_GW
    rows = min(SSD_TILE, s)
    xw, bw = gps * SSD_GW, gps * SSD_STATE
    nx = inner // xw
    nb = 2 * inner // bw
    nc = nb + g * SSD_STATE // bw
    cwb0 = inner // bw
    cwc0 = cwb0 + g * SSD_STATE // bw
    hps = gps * SSD_HPG

    dt_t = jnp.transpose(dt_raw)
    colv = lambda v: jnp.pad(v.astype(F32), (0, LANES - v.shape[0])).reshape(LANES, 1)
    conv_w = conv_w.astype(F32)
    conv_b = conv_b.reshape(1, -1).astype(F32)
    dsk = jnp.repeat(d_skip.astype(F32), SSD_HEAD_DIM).reshape(1, inner)
    ng = norm_g.reshape(1, inner).astype(F32)

    gs = lambda shape, fn: pl.BlockSpec(shape, fn)
    in_specs = [
        gs((rows, xw), lambda gi, t: (t, gi)),
        gs((rows, xw), lambda gi, t: (t, nx + gi)),
        gs((rows, bw), lambda gi, t: (t, nb + gi)),
        gs((rows, bw), lambda gi, t: (t, nc + gi)),
        gs((SSD_CONV, xw), lambda gi, t: (0, gi)),
        gs((SSD_CONV, bw), lambda gi, t: (0, cwb0 + gi)),
        gs((SSD_CONV, bw), lambda gi, t: (0, cwc0 + gi)),
        gs((1, xw), lambda gi, t: (0, gi)),
        gs((1, bw), lambda gi, t: (0, cwb0 + gi)),
        gs((1, bw), lambda gi, t: (0, cwc0 + gi)),
        gs((hps, rows), lambda gi, t: (gi, t)),
        gs((hps, 1), lambda gi, t: (gi, 0)),
        gs((hps, 1), lambda gi, t: (gi, 0)),
        gs((1, xw), lambda gi, t: (0, gi)),
        gs((1, xw), lambda gi, t: (0, gi)),
    ]
    return pl.pallas_call(
        _ssd_kernel,
        grid=(g // gps, s // rows),
        in_specs=in_specs,
        out_specs=pl.BlockSpec((rows, xw), lambda gi, t: (t, gi)),
        out_shape=jax.ShapeDtypeStruct((s, inner), BF16),
        scratch_shapes=[
            pltpu.VMEM((rows + HALO, xw), F32),
            pltpu.VMEM((rows + HALO, bw), F32),
            pltpu.VMEM((rows + HALO, bw), F32),
            pltpu.VMEM((SSD_STATE, xw), F32),
        ],
        compiler_params=_cparams(2),
        name="ssd_core",
    )(zx, zx, zx, zx, conv_w, conv_w, conv_w, conv_b, conv_b, conv_b,
      dt_t, colv(dt_bias), colv(a_log), dsk, ng)


def _ssd_layer(h, mix_norm, w_in, conv_w, conv_b, dt_bias, a_log, d_skip, norm_g, w_out):
    heads = dt_bias.shape[0]
    n_zx = w_in.shape[1] - heads
    w_t = jnp.transpose(w_in)
    w_dt = jnp.pad(w_t[n_zx:], ((0, LANES - heads), (0, 0)))
    zx, dt_raw = _matmul(h, w_t, norm_g=mix_norm, out_dtype=F32, n_cols=n_zx,
                         w_rows_are_outputs=True, w_extra=w_dt, name="ssd_in")
    y = _ssd_core(zx, dt_raw, conv_w, conv_b, dt_bias, a_log, d_skip, norm_g)
    return _matmul(y, w_out, residual=h, out_dtype=F32, tm=1024, name="ssd_out")


def _rope_kernel(qkv_ref, kw_ref, cos_ref, sin_ref, cosi_ref, sini_ref,
                 q_ref, k_ref, v_ref, qi_ref, ki_ref, w_ref, *, q_scale, w_scale):
    rows = qkv_ref.shape[0]
    lane = lax.broadcasted_iota(jnp.int32, (rows, LANES), 1)
    cos, sin = cos_ref[...], sin_ref[...]
    cosi, sini = cosi_ref[...], sini_ref[...]
    half_i = IDX_DIM // 2
    lower_i = (lane & (IDX_DIM - 1)) < half_i

    def rope_head(x):
        return x * cos + pltpu.roll(x, ATT_HEAD_DIM // 2, 1) * sin

    def rope_idx(x):
        rot = jnp.where(lower_i, pltpu.roll(x, LANES - half_i, 1), pltpu.roll(x, half_i, 1))
        return x * cosi + rot * sini

    nq = q_ref.shape[1] // LANES
    nkv = k_ref.shape[1] // LANES
    nqi = qi_ref.shape[1] // LANES
    for hh in range(nq):
        x = qkv_ref[:, hh * LANES:(hh + 1) * LANES]
        q_ref[:, hh * LANES:(hh + 1) * LANES] = (rope_head(x) * q_scale).astype(q_ref.dtype)
    off = nq
    for hh in range(nkv):
        x = qkv_ref[:, (off + hh) * LANES:(off + hh + 1) * LANES]
        k_ref[:, hh * LANES:(hh + 1) * LANES] = rope_head(x).astype(k_ref.dtype)
    off += nkv
    v_ref[...] = qkv_ref[:, off * LANES:(off + nkv) * LANES].astype(v_ref.dtype)
    off += nkv
    for hh in range(nqi):
        x = qkv_ref[:, (off + hh) * LANES:(off + hh + 1) * LANES]
        qi_ref[:, hh * LANES:(hh + 1) * LANES] = rope_idx(x).astype(qi_ref.dtype)
    kw = kw_ref[...]
    swapped = pltpu.roll(kw, LANES // 2, 1)
    ki_ref[...] = rope_idx(jnp.where(lane < IDX_DIM, kw, swapped)).astype(ki_ref.dtype)
    w_ref[...] = jnp.where(lane < IDX_HEADS, swapped * w_scale, 0.0)


def _rope_tables(seq):
    def table(dim):
        inv = ROPE_THETA ** (-jnp.arange(0, dim, 2, dtype=F32) / dim)
        ang = jnp.arange(seq, dtype=F32)[:, None] * inv[None, :]
        return jnp.cos(ang), jnp.sin(ang)

    c, s = table(ATT_HEAD_DIM)
    ci, si = table(IDX_DIM)
    return (jnp.concatenate([c, c], axis=1), jnp.concatenate([-s, s], axis=1),
            jnp.concatenate([ci, ci, ci, ci], axis=1), jnp.concatenate([-si, si, -si, si], axis=1))


def _dsa_rope(qkv, kw, t_rows=256):
    s = qkv.shape[0]
    t_rows = min(t_rows, s)
    dq = ATT_HEADS * ATT_HEAD_DIM
    dkv = ATT_KV_HEADS * ATT_HEAD_DIM
    dqi = IDX_HEADS * IDX_DIM
    cos, sin, cosi, sini = _rope_tables(s)
    tile = lambda width: pl.BlockSpec((t_rows, width), lambda t: (t, 0))
    return pl.pallas_call(
        functools.partial(_rope_kernel, q_scale=ATT_HEAD_DIM ** -0.5 * math.log2(math.e),
                          w_scale=IDX_HEADS ** -0.5 * IDX_DIM ** -0.5),
        grid=(s // t_rows,),
        in_specs=[tile(qkv.shape[1]), tile(LANES), tile(LANES), tile(LANES), tile(LANES), tile(LANES)],
        out_specs=[tile(dq), tile(dkv), tile(dkv), tile(dqi), tile(LANES), tile(LANES)],
        out_shape=[
            jax.ShapeDtypeStruct((s, dq), BF16),
            jax.ShapeDtypeStruct((s, dkv), BF16),
            jax.ShapeDtypeStruct((s, dkv), BF16),
            jax.ShapeDtypeStruct((s, dqi), BF16),
            jax.ShapeDtypeStruct((s, LANES), BF16),
            jax.ShapeDtypeStruct((s, LANES), F32),
        ],
        compiler_params=_cparams(1),
        name="dsa_rope",
    )(qkv, kw, cos, sin, cosi, sini)


DSA_TQ = 128
DSA_KB = 512


def _sortable_key(score):
    bits = lax.bitcast_convert_type(score, jnp.int32)
    return jnp.where(bits < 0, bits ^ jnp.int32(0x7FFFFFFF), bits)


def _key_hi(key):
    return jnp.right_shift(key, 16).astype(jnp.int16)


def _key_lo(key):
    return ((key & 0xFFFF) - 32768).astype(jnp.int16)


def _tile_lanes(v, width):
    return jnp.concatenate([v] * (width // v.shape[1]), axis=1)


def _dsa_kernel(q_ref, qi_ref, w_ref, k_ref, v_ref, ki_ref, o_ref,
                sc, hi16, lo16, qa, wb, qs, m_scr, acc_scr, *, top_k):
    i = pl.program_id(0)
    tq = q_ref.shape[0]
    kb_size = sc.shape[2]
    nkb = ((i + 1) * tq + kb_size - 1) // kb_size
    lane = lax.broadcasted_iota(jnp.int32, (tq, LANES), 1)
    chunk_shift = CHUNK.bit_length() - 1
    q_chunk = (i * tq + lax.broadcasted_iota(jnp.int32, (tq, kb_size), 0)) >> chunk_shift
    k_lane = lax.broadcasted_iota(jnp.int32, (tq, kb_size), 1)

    def admissible(kb):
        return ((kb * kb_size + k_lane) >> chunk_shift) <= q_chunk

    for h in range(IDX_HEADS):
        pair = qi_ref[:, (h // 2) * LANES:(h // 2 + 1) * LANES]
        keep = (lane < IDX_DIM) if h % 2 == 0 else (lane >= IDX_DIM)
        qa[h] = jnp.where(keep, pair, jnp.zeros_like(pair))
        wb[h] = jnp.broadcast_to(w_ref[:, h:h + 1], (tq, LANES))

    def score_block(kb, carry):
        kis = ki_ref[pl.ds(pl.multiple_of(kb * kb_size, kb_size), kb_size), :]
        score = jnp.zeros((tq, kb_size), F32)
        for h in range(IDX_HEADS):
            lg = _dot_nt(qa[h], kis)
            score = score + jnp.maximum(lg, 0.0) * _tile_lanes(wb[h], kb_size)
        key = jnp.where(admissible(kb), _sortable_key(score), jnp.int32(KEY_NEG_INF))
        sc[kb] = key
        hi16[kb] = _key_hi(key)
        lo16[kb] = _key_lo(key)
        return carry

    lax.fori_loop(0, nkb, score_block, 0)

    @pl.when(nkb % 2 == 1)
    def _():
        pad = jnp.full((tq, kb_size), KEY_NEG_INF, jnp.int32)
        hi16[nkb] = _key_hi(pad)
        lo16[nkb] = _key_lo(pad)

    n_pairs = (nkb + 1) // 2
    i16 = jnp.int16

    def count16(arr, cand, strict):
        cand_t = _tile_lanes(cand.astype(i16), kb_size)

        def count_pair(pi, acc):
            for u in range(2):
                blk = arr[2 * pi + u]
                hit = (blk > cand_t) if strict else (blk >= cand_t)
                one = jnp.where(hit, jnp.ones_like(blk), jnp.zeros_like(blk))
                for c in range(kb_size // LANES):
                    acc = acc + one[:, c * LANES:(c + 1) * LANES]
            return acc

        acc = lax.fori_loop(0, n_pairs, count_pair, jnp.zeros((tq, LANES), i16))
        cnt = jnp.sum(acc.astype(jnp.int32).astype(F32), axis=1, keepdims=True)
        return jnp.broadcast_to(cnt, (tq, LANES))

    def kth_largest16(arr, target):
        def bit_step(b, thr):
            cand = thr + jnp.left_shift(jnp.int32(1), 15 - b)
            return jnp.where(count16(arr, cand, False) >= target, cand, thr)

        return lax.fori_loop(0, 16, bit_step, jnp.full((tq, LANES), -32768, jnp.int32))

    k_f = jnp.full((tq, LANES), float(top_k), F32)
    thr_hi = kth_largest16(hi16, k_f)
    need = k_f - count16(hi16, thr_hi, True)
    thr_hi_t = _tile_lanes(thr_hi.astype(i16), kb_size)

    def tie_block(kb, carry):
        lo16[kb] = jnp.where(hi16[kb] == thr_hi_t, lo16[kb], jnp.full((tq, kb_size), -32768, i16))
        return carry

    lax.fori_loop(0, 2 * n_pairs, tie_block, 0)
    thr_lo = kth_largest16(lo16, need)
    thr = jnp.left_shift(thr_hi, 16) | ((thr_lo + 32768) & 0xFFFF)
    thr_t = _tile_lanes(thr, kb_size)

    n_above = count16(lo16, thr_lo, True)
    need_eq = need - n_above
    n_eq = count16(lo16, thr_lo, False) - n_above
    real_thr = thr > KEY_NEG_INF
    extra = jnp.where(jnp.logical_and(real_thr, n_eq > need_eq), 1.0, 0.0)
    has_extra_ties = jnp.max(extra) > 0.0

    @pl.when(jnp.logical_not(has_extra_ties))
    def _():
        def bias_block(kb, carry):
            sel = jnp.logical_and(sc[kb] >= thr_t, admissible(kb))
            sc[kb] = jnp.where(sel, jnp.int32(0), jnp.int32(NEG_INF_BITS))
            return carry

        lax.fori_loop(0, nkb, bias_block, 0)

    @pl.when(has_extra_ties)
    def _():
        ki_ = lax.broadcasted_iota(jnp.int32, (kb_size, kb_size), 0)
        kj_ = lax.broadcasted_iota(jnp.int32, (kb_size, kb_size), 1)
        before = jnp.where(ki_ < kj_, 1.0, 0.0).astype(BF16)
        limit = _tile_lanes(jnp.where(real_thr, need_eq, float(kb_size * sc.shape[0])), kb_size)

        def bias_block(kb, seen):
            key = sc[kb]
            adm = admissible(kb)
            eq = jnp.logical_and(key == thr_t, adm)
            eq_b = jnp.where(eq, 1.0, 0.0).astype(BF16)
            rank = _dot(eq_b, before) + _tile_lanes(seen, kb_size)
            keep = jnp.logical_or(jnp.logical_and(key > thr_t, adm),
                                  jnp.logical_and(eq, rank < limit))
            sc[kb] = jnp.where(keep, jnp.int32(0), jnp.int32(NEG_INF_BITS))
            total = _dot(eq_b, jnp.ones((kb_size, LANES), BF16))
            return seen + total

        lax.fori_loop(0, nkb, bias_block, jnp.zeros((tq, LANES), F32))

    for g in range(ATT_KV_HEADS):
        qs[g] = jnp.concatenate(
            [q_ref[:, (g * ATT_GROUP + a) * LANES:(g * ATT_GROUP + a + 1) * LANES]
             for a in range(ATT_GROUP)], axis=0)
    m_scr[...] = jnp.full(m_scr.shape, NEG_INF, F32)
    acc_scr[...] = jnp.zeros_like(acc_scr)
    ones = jnp.ones((kb_size, LANES), BF16)

    def attn_block(kb):
        start = pl.multiple_of(kb * kb_size, kb_size)
        bias = lax.bitcast_convert_type(sc[kb], F32)
        bias = jnp.concatenate([bias] * ATT_GROUP, axis=0)
        for g in range(ATT_KV_HEADS):
            kblk = k_ref[pl.ds(start, kb_size), g * LANES:(g + 1) * LANES]
            vblk = v_ref[pl.ds(start, kb_size), g * LANES:(g + 1) * LANES]
            s = _dot_nt(qs[g], kblk) + bias
            m_old = m_scr[g]
            m_new = jnp.maximum(m_old, jnp.max(s, axis=1, keepdims=True))
            m_safe = jnp.where(m_new == NEG_INF, 0.0, m_new)
            alpha = jnp.exp2(m_old - m_safe)
            p = jnp.exp2(s - _tile_lanes(m_safe, kb_size))
            pv = _dot(p.astype(BF16), jnp.concatenate([vblk, ones], axis=1))
            acc_scr[g] = _tile_lanes(alpha, 2 * LANES) * acc_scr[g] + pv
            m_scr[g] = m_new

    def attn_pair(pi, carry):
        attn_block(2 * pi)
        attn_block(2 * pi + 1)
        return carry

    lax.fori_loop(0, nkb // 2, attn_pair, 0)

    @pl.when(nkb % 2 == 1)
    def _():
        attn_block(nkb - 1)

    for g in range(ATT_KV_HEADS):
        out = acc_scr[g][:, :LANES] / acc_scr[g][:, LANES:]
        for a in range(ATT_GROUP):
            hh = g * ATT_GROUP + a
            o_ref[:, hh * LANES:(hh + 1) * LANES] = out[a * tq:(a + 1) * tq].astype(o_ref.dtype)


def _dsa_attend(q, k, v, qi, ki, w):
    s = q.shape[0]
    tq = min(DSA_TQ, s)
    kb = min(DSA_KB, s)
    top_k = min(TOPK_MAX, s // 4)
    resident = lambda width: pl.BlockSpec((s, width), lambda i: (0, 0),
                                          pipeline_mode=pl.Buffered(1))
    tile = lambda width: pl.BlockSpec((tq, width), lambda i: (i, 0))
    return pl.pallas_call(
        functools.partial(_dsa_kernel, top_k=top_k),
        grid=(s // tq,),
        in_specs=[tile(q.shape[1]), tile(qi.shape[1]), tile(LANES),
                  resident(k.shape[1]), resident(v.shape[1]), resident(LANES)],
        out_specs=tile(q.shape[1]),
        out_shape=jax.ShapeDtypeStruct(q.shape, BF16),
        scratch_shapes=[
            pltpu.VMEM((s // kb, tq, kb), jnp.int32),
            pltpu.VMEM((2 * pl.cdiv(s // kb, 2), tq, kb), jnp.int16),
            pltpu.VMEM((2 * pl.cdiv(s // kb, 2), tq, kb), jnp.int16),
            pltpu.VMEM((IDX_HEADS, tq, LANES), BF16),
            pltpu.VMEM((IDX_HEADS, tq, LANES), F32),
            pltpu.VMEM((ATT_KV_HEADS, ATT_GROUP * tq, LANES), BF16),
            pltpu.VMEM((ATT_KV_HEADS, ATT_GROUP * tq, LANES), F32),
            pltpu.VMEM((ATT_KV_HEADS, ATT_GROUP * tq, 2 * LANES), F32),
        ],
        compiler_params=_cparams(1),
        name="dsa_attend",
    )(q, qi, w, k, v, ki)


def _dsa_layer(h, mix_norm, w_in, w_out):
    n_main = (ATT_HEADS + 2 * ATT_KV_HEADS) * ATT_HEAD_DIM + IDX_HEADS * IDX_DIM
    n_rest = w_in.shape[1] - n_main
    w_t = jnp.transpose(w_in)
    w_rest = jnp.pad(w_t[n_main:], ((0, LANES - n_rest), (0, 0)))
    qkv, kw = _matmul(h, w_t, norm_g=mix_norm, out_dtype=F32, n_cols=n_main,
                      w_rows_are_outputs=True, w_extra=w_rest, name="dsa_in")
    q, k, v, qi, ki, w = _dsa_rope(qkv, kw)
    o = _dsa_attend(q, k, v, qi, ki, w)
    return _matmul(o, w_out, residual=h, out_dtype=F32, name="dsa_out")


def _mlp_kernel(x_ref, g_ref, wu_ref, wd_ref, *rest, has_out_norm):
    og_ref = rest[0] if has_out_norm else None
    o_ref, xn_ref = rest[-2:]
    f = pl.program_id(1)

    @pl.when(f == 0)
    def _():
        xf = x_ref[...]
        ms = jnp.mean(xf * xf, axis=-1, keepdims=True)
        xn_ref[...] = (xf * lax.rsqrt(ms + RMS_EPS) * g_ref[...]).astype(BF16)
        o_ref[...] = xf

    u = jnp.maximum(_dot(xn_ref[...], wu_ref[...].astype(BF16)), 0.0)
    o_ref[...] += _dot((u * u).astype(BF16), wd_ref[...].astype(BF16))

    if has_out_norm:
        @pl.when(f == pl.num_programs(1) - 1)
        def _():
            of = o_ref[...]
            ms = jnp.mean(of * of, axis=-1, keepdims=True)
            o_ref[...] = of * lax.rsqrt(ms + RMS_EPS) * og_ref[...]


def _mlp(h, norm_g, w_up, w_down, out_norm_g=None, tm=1024, tf=512):
    m, d = h.shape
    dff = w_up.shape[1]
    tm = min(tm, m)
    tf = min(tf, dff)
    row = lambda v: v.reshape(1, d).astype(F32)
    gain_spec = pl.BlockSpec((1, d), lambda i, f: (0, 0))
    in_specs = [
        pl.BlockSpec((tm, d), lambda i, f: (i, 0), pipeline_mode=pl.Buffered(1)),
        gain_spec,
        pl.BlockSpec((d, tf), lambda i, f: (0, f)),
        pl.BlockSpec((tf, d), lambda i, f: (f, 0)),
    ]
    args = [h, row(norm_g), w_up, w_down]
    if out_norm_g is not None:
        in_specs.append(gain_spec)
        args.append(row(out_norm_g))
    return pl.pallas_call(
        functools.partial(_mlp_kernel, has_out_norm=out_norm_g is not None),
        grid=(m // tm, dff // tf),
        in_specs=in_specs,
        out_specs=pl.BlockSpec((tm, d), lambda i, f: (i, 0)),
        out_shape=jax.ShapeDtypeStruct((m, d), F32),
        scratch_shapes=[pltpu.VMEM((tm, d), BF16)],
        compiler_params=_cparams(2),
        name="mlp",
    )(*args)


def kernel(x, l0_mix_norm, l0_rg_in_w, l0_rg_conv_w, l0_rg_conv_b, l0_rg_wa, l0_rg_ba, l0_rg_wx, l0_rg_bx, l0_rg_lambda, l0_rg_out_w, l0_mlp_norm, l0_mlp_up, l0_mlp_down, l1_mix_norm, l1_dsa_in_w, l1_dsa_out_w, l1_mlp_norm, l1_mlp_up, l1_mlp_down, l2_mix_norm, l2_ssd_in_w, l2_ssd_conv_w, l2_ssd_conv_b, l2_ssd_dt_bias, l2_ssd_a_log, l2_ssd_d, l2_ssd_norm, l2_ssd_out_w, l2_mlp_norm, l2_mlp_up, l2_mlp_down, l3_mix_norm, l3_rg_in_w, l3_rg_conv_w, l3_rg_conv_b, l3_rg_wa, l3_rg_ba, l3_rg_wx, l3_rg_bx, l3_rg_lambda, l3_rg_out_w, l3_mlp_norm, l3_mlp_up, l3_mlp_down, final_norm):
    bsz, seq, d = x.shape
    outs = []
    for b in range(bsz):
        h = x[b]
        h = _rglru_layer(h, l0_mix_norm, l0_rg_in_w, l0_rg_conv_w, l0_rg_conv_b, l0_rg_wa,
                         l0_rg_ba, l0_rg_wx, l0_rg_bx, l0_rg_lambda, l0_rg_out_w)
        h = _mlp(h, l0_mlp_norm, l0_mlp_up, l0_mlp_down)
        h = _dsa_layer(h, l1_mix_norm, l1_dsa_in_w, l1_dsa_out_w)
        h = _mlp(h, l1_mlp_norm, l1_mlp_up, l1_mlp_down)
        h = _ssd_layer(h, l2_mix_norm, l2_ssd_in_w, l2_ssd_conv_w, l2_ssd_conv_b, l2_ssd_dt_bias,
                       l2_ssd_a_log, l2_ssd_d, l2_ssd_norm, l2_ssd_out_w)
        h = _mlp(h, l2_mlp_norm, l2_mlp_up, l2_mlp_down)
        h = _rglru_layer(h, l3_mix_norm, l3_rg_in_w, l3_rg_conv_w, l3_rg_conv_b, l3_rg_wa,
                         l3_rg_ba, l3_rg_wx, l3_rg_bx, l3_rg_lambda, l3_rg_out_w)
        outs.append(_mlp(h, l3_mlp_norm, l3_mlp_up, l3_mlp_down, out_norm_g=final_norm))
    return jnp.stack(outs, axis=0)
```

```python
import functools
import math

import jax
import jax.numpy as jnp
from jax import lax
from jax.experimental import pallas as pl
from jax.experimental.pallas import tpu as pltpu

F32 = jnp.float32
BF16 = jnp.bfloat16

RMS_EPS = 1e-6
ROPE_THETA = 10000.0
CHUNK = 64
RG_BLOCKS = 8
RG_CONV = 4
RG_C = 8.0
ATT_HEADS = 16
ATT_KV_HEADS = 4
ATT_HEAD_DIM = 128
ATT_GROUP = ATT_HEADS // ATT_KV_HEADS
IDX_HEADS = 16
IDX_DIM = 64
TOPK_MAX = 256
SSD_HEAD_DIM = 64
SSD_GROUPS = 8
SSD_STATE = 128
SSD_CONV = 4

V7X_VMEM_BYTES = 64 * 1024 * 1024
VMEM_LIMIT_BYTES = V7X_VMEM_BYTES - 8 * 1024 * 1024
LANES = 128
SUBLANES = 8

NEG_INF = float("-inf")
KEY_NEG_INF = -2139095041
NEG_INF_BITS = -8388608


def _cparams(n_axes):
    return pltpu.CompilerParams(
        dimension_semantics=("arbitrary",) * n_axes,
        vmem_limit_bytes=VMEM_LIMIT_BYTES,
    )


def _dot(a, b):
    return jnp.dot(a, b, preferred_element_type=F32)


def _dot_nt(a, b):
    return lax.dot_general(a, b, (((1,), (1,)), ((), ())), preferred_element_type=F32)


def _dot_tn(a, b):
    return lax.dot_general(a, b, (((0,), (0,)), ((), ())), preferred_element_type=F32)


def _softplus(x):
    return jnp.maximum(x, 0.0) + jnp.log1p(jnp.exp(-jnp.abs(x)))


def _sigmoid(x):
    return jax.nn.sigmoid(x)


def _mm_kernel(*refs, has_norm, has_res, has_extra, w_rows_are_outputs):
    it = iter(refs)
    x_ref = next(it)
    g_ref = next(it) if has_norm else None
    w_ref = next(it)
    r_ref = next(it) if has_res else None
    we_ref = next(it) if has_extra else None
    o_ref = next(it)
    oe_ref = next(it) if has_extra else None
    xn_ref = next(it) if has_norm else None
    j = pl.program_id(1)

    if has_norm:
        @pl.when(j == 0)
        def _():
            xf = x_ref[...].astype(F32)
            ms = jnp.mean(xf * xf, axis=-1, keepdims=True)
            xn_ref[...] = (xf * lax.rsqrt(ms + RMS_EPS) * g_ref[...]).astype(BF16)

        lhs_ref = xn_ref
    else:
        lhs_ref = x_ref

    if has_extra:
        @pl.when(j == 0)
        def _():
            oe_ref[...] = _dot_nt(lhs_ref[...].astype(BF16), we_ref[...].astype(BF16))

    wt = w_ref[...].astype(BF16)
    lhs = lhs_ref[...].astype(BF16)
    v = _dot_nt(lhs, wt) if w_rows_are_outputs else _dot(lhs, wt)
    if has_res:
        v = v + r_ref[...]
    o_ref[...] = v.astype(o_ref.dtype)


def _matmul(x, w, *, norm_g=None, residual=None, out_dtype=F32, tm=2048, tn=512,
            n_cols=None, w_rows_are_outputs=False, w_extra=None, name="matmul"):
    m, kdim = x.shape
    n_all = w.shape[0] if w_rows_are_outputs else w.shape[1]
    n = n_all if n_cols is None else n_cols
    tm = min(tm, m)
    tn = min(tn, n)
    has_norm = norm_g is not None
    assert m % tm == 0 and n % tn == 0
    grid = (m // tm, n // tn)

    x_mode = {"pipeline_mode": pl.Buffered(1)} if x.dtype == F32 else {}
    in_specs = [pl.BlockSpec((tm, kdim), lambda i, j: (i, 0), **x_mode)]
    args = [x]
    if has_norm:
        in_specs.append(pl.BlockSpec((1, kdim), lambda i, j: (0, 0)))
        args.append(norm_g.reshape(1, kdim).astype(F32))
    if w_rows_are_outputs:
        in_specs.append(pl.BlockSpec((tn, kdim), lambda i, j: (j, 0)))
    else:
        in_specs.append(pl.BlockSpec((kdim, tn), lambda i, j: (0, j)))
    args.append(w)
    if residual is not None:
        in_specs.append(pl.BlockSpec((tm, tn), lambda i, j: (i, j)))
        args.append(residual)
    out_specs = pl.BlockSpec((tm, tn), lambda i, j: (i, j))
    out_shape = jax.ShapeDtypeStruct((m, n), out_dtype)
    if w_extra is not None:
        in_specs.append(pl.BlockSpec((LANES, kdim), lambda i, j: (0, 0)))
        args.append(w_extra)
        out_specs = [out_specs, pl.BlockSpec((tm, LANES), lambda i, j: (i, 0))]
        out_shape = [out_shape, jax.ShapeDtypeStruct((m, LANES), F32)]

    return pl.pallas_call(
        functools.partial(_mm_kernel, has_norm=has_norm, has_res=residual is not None,
                          has_extra=w_extra is not None,
                          w_rows_are_outputs=w_rows_are_outputs),
        grid=grid,
        in_specs=in_specs,
        out_specs=out_specs,
        out_shape=out_shape,
        scratch_shapes=[pltpu.VMEM((tm, kdim), BF16)] if has_norm else [],
        compiler_params=_cparams(2),
        name=name,
    )(*args)


HALO = SUBLANES


def _causal_conv(buf, x_ref, cw_ref, cb_ref, n_taps):
    t_rows = x_ref.shape[0]
    buf[HALO:HALO + t_rows, :] = x_ref[...]
    ext = buf[...]
    acc = cw_ref[0:1, :] * ext
    for k in range(1, n_taps):
        acc = pltpu.roll(acc, 1, 0) + cw_ref[k:k + 1, :] * ext
    buf[0:HALO, :] = buf[t_rows:t_rows + HALO, :]
    return cb_ref[...] + acc[HALO:, :]


def _rg_kernel(gate_ref, x_ref, cw_ref, cb_ref, wa_ref, wx_ref, ba_ref, bx_ref,
               lam_ref, y_ref, xbuf, a_scr, b_scr, hcar):
    t = pl.program_id(0)
    t_rows, width = x_ref.shape
    n_blocks, bw, _ = wa_ref.shape

    @pl.when(t == 0)
    def _():
        xbuf[0:HALO, :] = jnp.zeros((HALO, width), F32)
        hcar[...] = jnp.zeros_like(hcar)

    xc = _causal_conv(xbuf, x_ref, cw_ref, cb_ref, RG_CONV)
    xcb = xc.astype(BF16)
    sp = _softplus(-lam_ref[...])
    for c in range(n_blocks):
        sl = slice(c * bw, (c + 1) * bw)
        xs = xcb[:, sl]
        r = _sigmoid(_dot(xs, wa_ref[c]) + ba_ref[:, sl])
        i = _sigmoid(_dot(xs, wx_ref[c]) + bx_ref[:, sl])
        log_a = -RG_C * r * sp[:, sl]
        th = jnp.tanh(log_a)
        mult = jnp.sqrt(-2.0 * th / (1.0 - th))
        a_scr[:, sl] = jnp.exp(log_a)
        b_scr[:, sl] = mult * (i * xc[:, sl])

    def step(r, h):
        h = a_scr[pl.ds(r, 1), :] * h + b_scr[pl.ds(r, 1), :]
        b_scr[pl.ds(r, 1), :] = h
        return h

    hcar[...] = lax.fori_loop(0, t_rows, step, hcar[...], unroll=8)
    y_ref[...] = (b_scr[...] * jax.nn.gelu(gate_ref[...])).astype(y_ref.dtype)


def _rg_core(xg, conv_w, conv_b, wa, ba, wx, bx, lam, t_rows=256):
    s, two_w = xg.shape
    width = two_w // 2
    t_rows = min(t_rows, s)
    row = lambda v: v.reshape(1, width).astype(F32)
    full2 = lambda shape: pl.BlockSpec(shape, lambda t: (0, 0))
    full3 = lambda shape: pl.BlockSpec(shape, lambda t: (0, 0, 0))
    return pl.pallas_call(
        _rg_kernel,
        grid=(s // t_rows,),
        in_specs=[
            pl.BlockSpec((t_rows, width), lambda t: (t, 0)),
            pl.BlockSpec((t_rows, width), lambda t: (t, 1)),
            full2((RG_CONV, width)), full2((1, width)),
            full3(wa.shape), full3(wx.shape),
            full2((1, width)), full2((1, width)), full2((1, width)),
        ],
        out_specs=pl.BlockSpec((t_rows, width), lambda t: (t, 0)),
        out_shape=jax.ShapeDtypeStruct((s, width), BF16),
        scratch_shapes=[
            pltpu.VMEM((t_rows + HALO, width), F32),
            pltpu.VMEM((t_rows, width), F32),
            pltpu.VMEM((t_rows, width), F32),
            pltpu.VMEM((1, width), F32),
        ],
        compiler_params=_cparams(1),
        name="rg_core",
    )(xg, xg, conv_w.astype(F32), row(conv_b), wa.astype(BF16), wx.astype(BF16),
      row(ba), row(bx), row(lam))


def _rglru_layer(h, mix_norm, w_in, conv_w, conv_b, w_a, b_a, w_x, b_x, lam, w_out):
    xg = _matmul(h, w_in, norm_g=mix_norm, out_dtype=F32, name="rg_in")
    y = _rg_core(xg, conv_w, conv_b, w_a, b_a, w_x, b_x, lam)
    return _matmul(y, w_out, residual=h, out_dtype=F32, name="rg_out")


SSD_TILE = 128
SSD_HPG = 8
SSD_GW = SSD_HPG * SSD_HEAD_DIM
SSD_GPS = 8


def _cumsum(v, axis):
    n = v.shape[axis]
    idx = lax.broadcasted_iota(jnp.int32, v.shape, axis)
    shift = 1
    while shift < n:
        v = v + jnp.where(idx >= shift, pltpu.roll(v, shift, axis), 0.0)
        shift *= 2
    return v


def _expand_heads(cols, base):
    rows = cols.shape[0]
    lane = lax.broadcasted_iota(jnp.int32, (rows, LANES), 1)
    parts = []
    for v in range(SSD_GW // LANES):
        lo = jnp.broadcast_to(cols[:, base + 2 * v:base + 2 * v + 1], (rows, LANES))
        hi = jnp.broadcast_to(cols[:, base + 2 * v + 1:base + 2 * v + 2], (rows, LANES))
        parts.append(jnp.where(lane < SSD_HEAD_DIM, lo, hi))
    return jnp.concatenate(parts, axis=1)


def _ssd_kernel(z_ref, x_ref, b_ref, c_ref, cwx_ref, cwb_ref, cwc_ref, cbx_ref,
                cbb_ref, cbc_ref, dtr_ref, dbr_ref, alr_ref, dsk_ref, ng_ref,
                y_ref, xbuf, bbuf, cbuf, st):
    t = pl.program_id(1)
    rows = x_ref.shape[0]
    hpg, gw, ns = SSD_HPG, SSD_GW, SSD_STATE

    @pl.when(t == 0)
    def _():
        xbuf[0:HALO, :] = jnp.zeros((HALO, xbuf.shape[1]), F32)
        bbuf[0:HALO, :] = jnp.zeros((HALO, bbuf.shape[1]), F32)
        cbuf[0:HALO, :] = jnp.zeros((HALO, cbuf.shape[1]), F32)
        st[...] = jnp.zeros_like(st)

    def conv_silu(buf, ref, cw, cb):
        v = _causal_conv(buf, ref, cw, cb, SSD_CONV)
        return v * _sigmoid(v)

    xs_all = conv_silu(xbuf, x_ref, cwx_ref, cbx_ref)
    bm_all = conv_silu(bbuf, b_ref, cwb_ref, cbb_ref)
    cm_all = conv_silu(cbuf, c_ref, cwc_ref, cbc_ref)

    li = lax.broadcasted_iota(jnp.int32, (rows, rows), 0)
    si = lax.broadcasted_iota(jnp.int32, (rows, rows), 1)
    causal = li >= si
    eye = jnp.where(li == si, 1.0, 0.0)
    lane = lax.broadcasted_iota(jnp.int32, (rows, LANES), 1)

    groups = range(x_ref.shape[1] // gw)
    gsl = [slice(gi * gw, (gi + 1) * gw) for gi in groups]
    xs = [xs_all[:, gsl[gi]] for gi in groups]
    bmb = [bm_all[:, gi * ns:(gi + 1) * ns].astype(BF16) for gi in groups]
    cmb = [cm_all[:, gi * ns:(gi + 1) * ns].astype(BF16) for gi in groups]
    cb = [_dot_nt(cmb[gi], bmb[gi]) for gi in groups]
    stg = [st[:, gsl[gi]] for gi in groups]
    y_off = [_dot(cmb[gi], stg[gi].astype(BF16)) for gi in groups]

    csr, both = [], []
    for gi in groups:
        hs = slice(gi * hpg, (gi + 1) * hpg)
        dtr = _softplus(dtr_ref[hs, :] + dbr_ref[hs, :])
        csr.append(_cumsum(dtr * (-jnp.exp(alr_ref[hs, :])), 1))
        both.append(lax.dot_general(eye, jnp.concatenate([dtr, csr[gi]], axis=0),
                                    (((1,), (1,)), ((), ())),
                                    precision=lax.Precision.HIGHEST,
                                    preferred_element_type=F32))
    last = [both[gi][rows - 1:rows, :] for gi in groups]

    xdt = [xs[gi] * _expand_heads(both[gi], 0) for gi in groups]
    xdtb = [xdt[gi].astype(BF16) for gi in groups]
    xw = [(xdt[gi] * _expand_heads(jnp.exp(last[gi] - both[gi]), hpg)).astype(BF16)
          for gi in groups]
    st_new = [_dot_tn(bmb[gi], xw[gi]) for gi in groups]

    ys = [[] for _ in groups]
    for v in range(gw // LANES):
        pair = [[] for _ in groups]
        for j in (2 * v, 2 * v + 1):
            for gi in groups:
                seg = both[gi][:, hpg + j:hpg + j + 1] - csr[gi][j:j + 1, :]
                dec = jnp.exp(jnp.where(causal, seg, NEG_INF))
                pair[gi].append(_dot((cb[gi] * dec).astype(BF16),
                                     xdtb[gi][:, v * LANES:(v + 1) * LANES]))
        for gi in groups:
            ys[gi].append(jnp.where(lane < SSD_HEAD_DIM, pair[gi][0], pair[gi][1]))

    for gi in groups:
        y = jnp.concatenate(ys[gi], axis=1)
        y = y + y_off[gi] * _expand_heads(jnp.exp(both[gi]), hpg)
        y = y + xs[gi] * dsk_ref[:, gsl[gi]]
        z = z_ref[:, gsl[gi]]
        yz = y * (z * _sigmoid(z))
        yz = yz * lax.rsqrt(jnp.mean(yz * yz, axis=-1, keepdims=True) + RMS_EPS)
        y_ref[:, gsl[gi]] = (yz * ng_ref[:, gsl[gi]]).astype(y_ref.dtype)
        st[:, gsl[gi]] = stg[gi] * _expand_heads(jnp.exp(last[gi]), hpg) + st_new[gi]


def _ssd_core(zx, dt_raw, conv_w, conv_b, dt_bias, a_log, d_skip, norm_g):
    s = zx.shape[0]
    g = SSD_GROUPS
    gps = SSD_GPS
    inner = g * SSD_GW
    rows = min(SSD_TILE, s)
    xw, bw = gps * SSD_GW, gps * SSD_STATE
    nx = inner // xw
    nb = 2 * inner // bw
    nc = nb + g * SSD_STATE // bw
    cwb0 = inner // bw
    cwc0 = cwb0 + g * SSD_STATE // bw
    hps = gps * SSD_HPG

    dt_t = jnp.transpose(dt_raw)
    colv = lambda v: jnp.pad(v.astype(F32), (0, LANES - v.shape[0])).reshape(LANES, 1)
    conv_w = conv_w.astype(F32)
    conv_b = conv_b.reshape(1, -1).astype(F32)
    dsk = jnp.repeat(d_skip.astype(F32), SSD_HEAD_DIM).reshape(1, inner)
    ng = norm_g.reshape(1, inner).astype(F32)

    gs = lambda shape, fn: pl.BlockSpec(shape, fn)
    in_specs = [
        gs((rows, xw), lambda gi, t: (t, gi)),
        gs((rows, xw), lambda gi, t: (t, nx + gi)),
        gs((rows, bw), lambda gi, t: (t, nb + gi)),
        gs((rows, bw), lambda gi, t: (t, nc + gi)),
        gs((SSD_CONV, xw), lambda gi, t: (0, gi)),
        gs((SSD_CONV, bw), lambda gi, t: (0, cwb0 + gi)),
        gs((SSD_CONV, bw), lambda gi, t: (0, cwc0 + gi)),
        gs((1, xw), lambda gi, t: (0, gi)),
        gs((1, bw), lambda gi, t: (0, cwb0 + gi)),
        gs((1, bw), lambda gi, t: (0, cwc0 + gi)),
        gs((hps, rows), lambda gi, t: (gi, t)),
        gs((hps, 1), lambda gi, t: (gi, 0)),
        gs((hps, 1), lambda gi, t: (gi, 0)),
        gs((1, xw), lambda gi, t: (0, gi)),
        gs((1, xw), lambda gi, t: (0, gi)),
    ]
    return pl.pallas_call(
        _ssd_kernel,
        grid=(g // gps, s // rows),
        in_specs=in_specs,
        out_specs=pl.BlockSpec((rows, xw), lambda gi, t: (t, gi)),
        out_shape=jax.ShapeDtypeStruct((s, inner), BF16),
        scratch_shapes=[
            pltpu.VMEM((rows + HALO, xw), F32),
            pltpu.VMEM((rows + HALO, bw), F32),
            pltpu.VMEM((rows + HALO, bw), F32),
            pltpu.VMEM((SSD_STATE, xw), F32),
        ],
        compiler_params=_cparams(2),
        name="ssd_core",
    )(zx, zx, zx, zx, conv_w, conv_w, conv_w, conv_b, conv_b, conv_b,
      dt_t, colv(dt_bias), colv(a_log), dsk, ng)


def _ssd_layer(h, mix_norm, w_in, conv_w, conv_b, dt_bias, a_log, d_skip, norm_g, w_out):
    heads = dt_bias.shape[0]
    n_zx = w_in.shape[1] - heads
    w_t = jnp.transpose(w_in)
    w_dt = jnp.pad(w_t[n_zx:], ((0, LANES - heads), (0, 0)))
    zx, dt_raw = _matmul(h, w_t, norm_g=mix_norm, out_dtype=F32, n_cols=n_zx,
                         w_rows_are_outputs=True, w_extra=w_dt, name="ssd_in")
    y = _ssd_core(zx, dt_raw, conv_w, conv_b, dt_bias, a_log, d_skip, norm_g)
    return _matmul(y, w_out, residual=h, out_dtype=F32, tm=1024, name="ssd_out")


def _rope_kernel(qkv_ref, kw_ref, cos_ref, sin_ref, cosi_ref, sini_ref,
                 q_ref, k_ref, v_ref, qi_ref, ki_ref, w_ref, *, q_scale, w_scale):
    rows = qkv_ref.shape[0]
    lane = lax.broadcasted_iota(jnp.int32, (rows, LANES), 1)
    cos, sin = cos_ref[...], sin_ref[...]
    cosi, sini = cosi_ref[...], sini_ref[...]
    half_i = IDX_DIM // 2
    lower_i = (lane & (IDX_DIM - 1)) < half_i

    def rope_head(x):
        return x * cos + pltpu.roll(x, ATT_HEAD_DIM // 2, 1) * sin

    def rope_idx(x):
        rot = jnp.where(lower_i, pltpu.roll(x, LANES - half_i, 1), pltpu.roll(x, half_i, 1))
        return x * cosi + rot * sini

    nq = q_ref.shape[1] // LANES
    nkv = k_ref.shape[1] // LANES
    nqi = qi_ref.shape[1] // LANES
    for hh in range(nq):
        x = qkv_ref[:, hh * LANES:(hh + 1) * LANES]
        q_ref[:, hh * LANES:(hh + 1) * LANES] = (rope_head(x) * q_scale).astype(q_ref.dtype)
    off = nq
    for hh in range(nkv):
        x = qkv_ref[:, (off + hh) * LANES:(off + hh + 1) * LANES]
        k_ref[:, hh * LANES:(hh + 1) * LANES] = rope_head(x).astype(k_ref.dtype)
    off += nkv
    v_ref[...] = qkv_ref[:, off * LANES:(off + nkv) * LANES].astype(v_ref.dtype)
    off += nkv
    for hh in range(nqi):
        x = qkv_ref[:, (off + hh) * LANES:(off + hh + 1) * LANES]
        qi_ref[:, hh * LANES:(hh + 1) * LANES] = rope_idx(x).astype(qi_ref.dtype)
    kw = kw_ref[...]
    swapped = pltpu.roll(kw, LANES // 2, 1)
    ki_ref[...] = rope_idx(jnp.where(lane < IDX_DIM, kw, swapped)).astype(ki_ref.dtype)
    w_ref[...] = jnp.where(lane < IDX_HEADS, swapped * w_scale, 0.0)


def _rope_tables(seq):
    def table(dim):
        inv = ROPE_THETA ** (-jnp.arange(0, dim, 2, dtype=F32) / dim)
        ang = jnp.arange(seq, dtype=F32)[:, None] * inv[None, :]
        return jnp.cos(ang), jnp.sin(ang)

    c, s = table(ATT_HEAD_DIM)
    ci, si = table(IDX_DIM)
    return (jnp.concatenate([c, c], axis=1), jnp.concatenate([-s, s], axis=1),
            jnp.concatenate([ci, ci, ci, ci], axis=1), jnp.concatenate([-si, si, -si, si], axis=1))


def _dsa_rope(qkv, kw, t_rows=256):
    s = qkv.shape[0]
    t_rows = min(t_rows, s)
    dq = ATT_HEADS * ATT_HEAD_DIM
    dkv = ATT_KV_HEADS * ATT_HEAD_DIM
    dqi = IDX_HEADS * IDX_DIM
    cos, sin, cosi, sini = _rope_tables(s)
    tile = lambda width: pl.BlockSpec((t_rows, width), lambda t: (t, 0))
    return pl.pallas_call(
        functools.partial(_rope_kernel, q_scale=ATT_HEAD_DIM ** -0.5 * math.log2(math.e),
                          w_scale=IDX_HEADS ** -0.5 * IDX_DIM ** -0.5),
        grid=(s // t_rows,),
        in_specs=[tile(qkv.shape[1]), tile(LANES), tile(LANES), tile(LANES), tile(LANES), tile(LANES)],
        out_specs=[tile(dq), tile(dkv), tile(dkv), tile(dqi), tile(LANES), tile(LANES)],
        out_shape=[
            jax.ShapeDtypeStruct((s, dq), BF16),
            jax.ShapeDtypeStruct((s, dkv), BF16),
            jax.ShapeDtypeStruct((s, dkv), BF16),
            jax.ShapeDtypeStruct((s, dqi), BF16),
            jax.ShapeDtypeStruct((s, LANES), BF16),
            jax.ShapeDtypeStruct((s, LANES), F32),
        ],
        compiler_params=_cparams(1),
        name="dsa_rope",
    )(qkv, kw, cos, sin, cosi, sini)


DSA_TQ = 128
DSA_KB = 512


def _sortable_key(score):
    bits = lax.bitcast_convert_type(score, jnp.int32)
    return jnp.where(bits < 0, bits ^ jnp.int32(0x7FFFFFFF), bits)


def _key_hi(key):
    return jnp.right_shift(key, 16).astype(jnp.int16)


def _key_lo(key):
    return ((key & 0xFFFF) - 32768).astype(jnp.int16)


def _tile_lanes(v, width):
    return jnp.concatenate([v] * (width // v.shape[1]), axis=1)


def _dsa_kernel(q_ref, qi_ref, w_ref, k_ref, v_ref, ki_ref, o_ref,
                sc, hi16, lo16, qa, wb, qs, m_scr, acc_scr, *, top_k):
    i = pl.program_id(0)
    tq = q_ref.shape[0]
    kb_size = sc.shape[2]
    nkb = ((i + 1) * tq + kb_size - 1) // kb_size
    lane = lax.broadcasted_iota(jnp.int32, (tq, LANES), 1)
    chunk_shift = CHUNK.bit_length() - 1
    q_chunk = (i * tq + lax.broadcasted_iota(jnp.int32, (tq, kb_size), 0)) >> chunk_shift
    k_lane = lax.broadcasted_iota(jnp.int32, (tq, kb_size), 1)

    def admissible(kb):
        return ((kb * kb_size + k_lane) >> chunk_shift) <= q_chunk

    for h in range(IDX_HEADS):
        pair = qi_ref[:, (h // 2) * LANES:(h // 2 + 1) * LANES]
        keep = (lane < IDX_DIM) if h % 2 == 0 else (lane >= IDX_DIM)
        qa[h] = jnp.where(keep, pair, jnp.zeros_like(pair))
        wb[h] = jnp.broadcast_to(w_ref[:, h:h + 1], (tq, LANES))

    def score_block(kb, carry):
        kis = ki_ref[pl.ds(pl.multiple_of(kb * kb_size, kb_size), kb_size), :]
        score = jnp.zeros((tq, kb_size), F32)
        for h in range(IDX_HEADS):
            lg = _dot_nt(qa[h], kis)
            score = score + jnp.maximum(lg, 0.0) * _tile_lanes(wb[h], kb_size)
        key = jnp.where(admissible(kb), _sortable_key(score), jnp.int32(KEY_NEG_INF))
        sc[kb] = key
        hi16[kb] = _key_hi(key)
        lo16[kb] = _key_lo(key)
        return carry

    lax.fori_loop(0, nkb, score_block, 0)

    @pl.when(nkb % 2 == 1)
    def _():
        pad = jnp.full((tq, kb_size), KEY_NEG_INF, jnp.int32)
        hi16[nkb] = _key_hi(pad)
        lo16[nkb] = _key_lo(pad)

    n_pairs = (nkb + 1) // 2
    i16 = jnp.int16

    def count16(arr, cand, strict):
        cand_t = _tile_lanes(cand.astype(i16), kb_size)

        def count_pair(pi, acc):
            for u in range(2):
                blk = arr[2 * pi + u]
                hit = (blk > cand_t) if strict else (blk >= cand_t)
                one = jnp.where(hit, jnp.ones_like(blk), jnp.zeros_like(blk))
                for c in range(kb_size // LANES):
                    acc = acc + one[:, c * LANES:(c + 1) * LANES]
            return acc

        acc = lax.fori_loop(0, n_pairs, count_pair, jnp.zeros((tq, LANES), i16))
        cnt = jnp.sum(acc.astype(jnp.int32).astype(F32), axis=1, keepdims=True)
        return jnp.broadcast_to(cnt, (tq, LANES))

    def kth_largest16(arr, target):
        def bit_step(b, thr):
            cand = thr + jnp.left_shift(jnp.int32(1), 15 - b)
            return jnp.where(count16(arr, cand, False) >= target, cand, thr)

        return lax.fori_loop(0, 16, bit_step, jnp.full((tq, LANES), -32768, jnp.int32))

    k_f = jnp.full((tq, LANES), float(top_k), F32)
    thr_hi = kth_largest16(hi16, k_f)
    need = k_f - count16(hi16, thr_hi, True)
    thr_hi_t = _tile_lanes(thr_hi.astype(i16), kb_size)

    def tie_block(kb, carry):
        lo16[kb] = jnp.where(hi16[kb] == thr_hi_t, lo16[kb], jnp.full((tq, kb_size), -32768, i16))
        return carry

    lax.fori_loop(0, 2 * n_pairs, tie_block, 0)
    thr_lo = kth_largest16(lo16, need)
    thr = jnp.left_shift(thr_hi, 16) | ((thr_lo + 32768) & 0xFFFF)
    thr_t = _tile_lanes(thr, kb_size)

    n_above = count16(lo16, thr_lo, True)
    need_eq = need - n_above
    n_eq = count16(lo16, thr_lo, False) - n_above
    real_thr = thr > KEY_NEG_INF
    extra = jnp.where(jnp.logical_and(real_thr, n_eq > need_eq), 1.0, 0.0)
    has_extra_ties = jnp.max(extra) > 0.0

    @pl.when(jnp.logical_not(has_extra_ties))
    def _():
        def bias_block(kb, carry):
            sel = jnp.logical_and(sc[kb] >= thr_t, admissible(kb))
            sc[kb] = jnp.where(sel, jnp.int32(0), jnp.int32(NEG_INF_BITS))
            return carry

        lax.fori_loop(0, nkb, bias_block, 0)

    @pl.when(has_extra_ties)
    def _():
        ki_ = lax.broadcasted_iota(jnp.int32, (kb_size, kb_size), 0)
        kj_ = lax.broadcasted_iota(jnp.int32, (kb_size, kb_size), 1)
        before = jnp.where(ki_ < kj_, 1.0, 0.0).astype(BF16)
        limit = _tile_lanes(jnp.where(real_thr, need_eq, float(kb_size * sc.shape[0])), kb_size)

        def bias_block(kb, seen):
            key = sc[kb]
            adm = admissible(kb)
            eq = jnp.logical_and(key == thr_t, adm)
            eq_b = jnp.where(eq, 1.0, 0.0).astype(BF16)
            rank = _dot(eq_b, before) + _tile_lanes(seen, kb_size)
            keep = jnp.logical_or(jnp.logical_and(key > thr_t, adm),
                                  jnp.logical_and(eq, rank < limit))
            sc[kb] = jnp.where(keep, jnp.int32(0), jnp.int32(NEG_INF_BITS))
            total = _dot(eq_b, jnp.ones((kb_size, LANES), BF16))
            return seen + total

        lax.fori_loop(0, nkb, bias_block, jnp.zeros((tq, LANES), F32))

    for g in range(ATT_KV_HEADS):
        qs[g] = jnp.concatenate(
            [q_ref[:, (g * ATT_GROUP + a) * LANES:(g * ATT_GROUP + a + 1) * LANES]
             for a in range(ATT_GROUP)], axis=0)
    m_scr[...] = jnp.full(m_scr.shape, NEG_INF, F32)
    acc_scr[...] = jnp.zeros_like(acc_scr)
    ones = jnp.ones((kb_size, LANES), BF16)

    def attn_block(kb):
        start = pl.multiple_of(kb * kb_size, kb_size)
        bias = lax.bitcast_convert_type(sc[kb], F32)
        bias = jnp.concatenate([bias] * ATT_GROUP, axis=0)
        for g in range(ATT_KV_HEADS):
            kblk = k_ref[pl.ds(start, kb_size), g * LANES:(g + 1) * LANES]
            vblk = v_ref[pl.ds(start, kb_size), g * LANES:(g + 1) * LANES]
            s = _dot_nt(qs[g], kblk) + bias
            m_old = m_scr[g]
            m_new = jnp.maximum(m_old, jnp.max(s, axis=1, keepdims=True))
            m_safe = jnp.where(m_new == NEG_INF, 0.0, m_new)
            alpha = jnp.exp2(m_old - m_safe)
            p = jnp.exp2(s - _tile_lanes(m_safe, kb_size))
            pv = _dot(p.astype(BF16), jnp.concatenate([vblk, ones], axis=1))
            acc_scr[g] = _tile_lanes(alpha, 2 * LANES) * acc_scr[g] + pv
            m_scr[g] = m_new

    def attn_pair(pi, carry):
        attn_block(2 * pi)
        attn_block(2 * pi + 1)
        return carry

    lax.fori_loop(0, nkb // 2, attn_pair, 0)

    @pl.when(nkb % 2 == 1)
    def _():
        attn_block(nkb - 1)

    for g in range(ATT_KV_HEADS):
        out = acc_scr[g][:, :LANES] / acc_scr[g][:, LANES:]
        for a in range(ATT_GROUP):
            hh = g * ATT_GROUP + a
            o_ref[:, hh * LANES:(hh + 1) * LANES] = out[a * tq:(a + 1) * tq].astype(o_ref.dtype)


def _dsa_attend(q, k, v, qi, ki, w):
    s = q.shape[0]
    tq = min(DSA_TQ, s)
    kb = min(DSA_KB, s)
    top_k = min(TOPK_MAX, s // 4)
    resident = lambda width: pl.BlockSpec((s, width), lambda i: (0, 0),
                                          pipeline_mode=pl.Buffered(1))
    tile = lambda width: pl.BlockSpec((tq, width), lambda i: (i, 0))
    return pl.pallas_call(
        functools.partial(_dsa_kernel, top_k=top_k),
        grid=(s // tq,),
        in_specs=[tile(q.shape[1]), tile(qi.shape[1]), tile(LANES),
                  resident(k.shape[1]), resident(v.shape[1]), resident(LANES)],
        out_specs=tile(q.shape[1]),
        out_shape=jax.ShapeDtypeStruct(q.shape, BF16),
        scratch_shapes=[
            pltpu.VMEM((s // kb, tq, kb), jnp.int32),
            pltpu.VMEM((2 * pl.cdiv(s // kb, 2), tq, kb), jnp.int16),
            pltpu.VMEM((2 * pl.cdiv(s // kb, 2), tq, kb), jnp.int16),
            pltpu.VMEM((IDX_HEADS, tq, LANES), BF16),
            pltpu.VMEM((IDX_HEADS, tq, LANES), F32),
            pltpu.VMEM((ATT_KV_HEADS, ATT_GROUP * tq, LANES), BF16),
            pltpu.VMEM((ATT_KV_HEADS, ATT_GROUP * tq, LANES), F32),
            pltpu.VMEM((ATT_KV_HEADS, ATT_GROUP * tq, 2 * LANES), F32),
        ],
        compiler_params=_cparams(1),
        name="dsa_attend",
    )(q, qi, w, k, v, ki)


def _dsa_layer(h, mix_norm, w_in, w_out):
    n_main = (ATT_HEADS + 2 * ATT_KV_HEADS) * ATT_HEAD_DIM + IDX_HEADS * IDX_DIM
    n_rest = w_in.shape[1] - n_main
    w_t = jnp.transpose(w_in)
    w_rest = jnp.pad(w_t[n_main:], ((0, LANES - n_rest), (0, 0)))
    qkv, kw = _matmul(h, w_t, norm_g=mix_norm, out_dtype=F32, n_cols=n_main,
                      w_rows_are_outputs=True, w_extra=w_rest, name="dsa_in")
    q, k, v, qi, ki, w = _dsa_rope(qkv, kw)
    o = _dsa_attend(q, k, v, qi, ki, w)
    return _matmul(o, w_out, residual=h, out_dtype=F32, name="dsa_out")


def _mlp_kernel(x_ref, g_ref, wu_ref, wd_ref, *rest, has_out_norm):
    og_ref = rest[0] if has_out_norm else None
    o_ref, xn_ref = rest[-2:]
    f = pl.program_id(1)

    @pl.when(f == 0)
    def _():
        xf = x_ref[...]
        ms = jnp.mean(xf * xf, axis=-1, keepdims=True)
        xn_ref[...] = (xf * lax.rsqrt(ms + RMS_EPS) * g_ref[...]).astype(BF16)
        o_ref[...] = xf

    u = jnp.maximum(_dot(xn_ref[...], wu_ref[...].astype(BF16)), 0.0)
    o_ref[...] += _dot((u * u).astype(BF16), wd_ref[...].astype(BF16))

    if has_out_norm:
        @pl.when(f == pl.num_programs(1) - 1)
        def _():
            of = o_ref[...]
            ms = jnp.mean(of * of, axis=-1, keepdims=True)
            o_ref[...] = of * lax.rsqrt(ms + RMS_EPS) * og_ref[...]


def _mlp(h, norm_g, w_up, w_down, out_norm_g=None, tm=1024, tf=512):
    m, d = h.shape
    dff = w_up.shape[1]
    tm = min(tm, m)
    tf = min(tf, dff)
    row = lambda v: v.reshape(1, d).astype(F32)
    gain_spec = pl.BlockSpec((1, d), lambda i, f: (0, 0))
    in_specs = [
        pl.BlockSpec((tm, d), lambda i, f: (i, 0), pipeline_mode=pl.Buffered(1)),
        gain_spec,
        pl.BlockSpec((d, tf), lambda i, f: (0, f)),
        pl.BlockSpec((tf, d), lambda i, f: (f, 0)),
    ]
    args = [h, row(norm_g), w_up, w_down]
    if out_norm_g is not None:
        in_specs.append(gain_spec)
        args.append(row(out_norm_g))
    return pl.pallas_call(
        functools.partial(_mlp_kernel, has_out_norm=out_norm_g is not None),
        grid=(m // tm, dff // tf),
        in_specs=in_specs,
        out_specs=pl.BlockSpec((tm, d), lambda i, f: (i, 0)),
        out_shape=jax.ShapeDtypeStruct((m, d), F32),
        scratch_shapes=[pltpu.VMEM((tm, d), BF16)],
        compiler_params=_cparams(2),
        name="mlp",
    )(*args)


def kernel(x, l0_mix_norm, l0_rg_in_w, l0_rg_conv_w, l0_rg_conv_b, l0_rg_wa, l0_rg_ba, l0_rg_wx, l0_rg_bx, l0_rg_lambda, l0_rg_out_w, l0_mlp_norm, l0_mlp_up, l0_mlp_down, l1_mix_norm, l1_dsa_in_w, l1_dsa_out_w, l1_mlp_norm, l1_mlp_up, l1_mlp_down, l2_mix_norm, l2_ssd_in_w, l2_ssd_conv_w, l2_ssd_conv_b, l2_ssd_dt_bias, l2_ssd_a_log, l2_ssd_d, l2_ssd_norm, l2_ssd_out_w, l2_mlp_norm, l2_mlp_up, l2_mlp_down, l3_mix_norm, l3_rg_in_w, l3_rg_conv_w, l3_rg_conv_b, l3_rg_wa, l3_rg_ba, l3_rg_wx, l3_rg_bx, l3_rg_lambda, l3_rg_out_w, l3_mlp_norm, l3_mlp_up, l3_mlp_down, final_norm):
    bsz, seq, d = x.shape
    outs = []
    for b in range(bsz):
        h = x[b]
        h = _rglru_layer(h, l0_mix_norm, l0_rg_in_w, l0_rg_conv_w, l0_rg_conv_b, l0_rg_wa,
                         l0_rg_ba, l0_rg_wx, l0_rg_bx, l0_rg_lambda, l0_rg_out_w)
        h = _mlp(h, l0_mlp_norm, l0_mlp_up, l0_mlp_down)
        h = _dsa_layer(h, l1_mix_norm, l1_dsa_in_w, l1_dsa_out_w)
        h = _mlp(h, l1_mlp_norm, l1_mlp_up, l1_mlp_down)
        h = _ssd_layer(h, l2_mix_norm, l2_ssd_in_w, l2_ssd_conv_w, l2_ssd_conv_b, l2_ssd_dt_bias,
                       l2_ssd_a_log, l2_ssd_d, l2_ssd_norm, l2_ssd_out_w)
        h = _mlp(h, l2_mlp_norm, l2_mlp_up, l2_mlp_down)
        h = _rglru_layer(h, l3_mix_norm, l3_rg_in_w, l3_rg_conv_w, l3_rg_conv_b, l3_rg_wa,
                         l3_rg_ba, l3_rg_wx, l3_rg_bx, l3_rg_lambda, l3_rg_out_w)
        outs.append(_mlp(h, l3_mlp_norm, l3_mlp_up, l3_mlp_down, out_norm_g=final_norm))
    return jnp.stack(outs, axis=0)
```

```python
import functools
import math

import jax
import jax.numpy as jnp
from jax import lax
from jax.experimental import pallas as pl
from jax.experimental.pallas import tpu as pltpu

F32 = jnp.float32
BF16 = jnp.bfloat16

RMS_EPS = 1e-6
ROPE_THETA = 10000.0
CHUNK = 64
RG_BLOCKS = 8
RG_CONV = 4
RG_C = 8.0
ATT_HEADS = 16
ATT_KV_HEADS = 4
ATT_HEAD_DIM = 128
ATT_GROUP = ATT_HEADS // ATT_KV_HEADS
IDX_HEADS = 16
IDX_DIM = 64
TOPK_MAX = 256
SSD_HEAD_DIM = 64
SSD_GROUPS = 8
SSD_STATE = 128
SSD_CONV = 4

V7X_VMEM_BYTES = 64 * 1024 * 1024
VMEM_LIMIT_BYTES = V7X_VMEM_BYTES - 8 * 1024 * 1024
LANES = 128
SUBLANES = 8

NEG_INF = float("-inf")
KEY_NEG_INF = -2139095041
NEG_INF_BITS = -8388608


def _cparams(n_axes):
    return pltpu.CompilerParams(
        dimension_semantics=("arbitrary",) * n_axes,
        vmem_limit_bytes=VMEM_LIMIT_BYTES,
    )


def _dot(a, b):
    return jnp.dot(a, b, preferred_element_type=F32)


def _dot_nt(a, b):
    return lax.dot_general(a, b, (((1,), (1,)), ((), ())), preferred_element_type=F32)


def _dot_tn(a, b):
    return lax.dot_general(a, b, (((0,), (0,)), ((), ())), preferred_element_type=F32)


def _softplus(x):
    return jnp.maximum(x, 0.0) + jnp.log1p(jnp.exp(-jnp.abs(x)))


def _sigmoid(x):
    return jax.nn.sigmoid(x)


def _mm_kernel(*refs, has_norm, has_res, has_extra, w_rows_are_outputs):
    it = iter(refs)
    x_ref = next(it)
    g_ref = next(it) if has_norm else None
    w_ref = next(it)
    r_ref = next(it) if has_res else None
    we_ref = next(it) if has_extra else None
    o_ref = next(it)
    oe_ref = next(it) if has_extra else None
    xn_ref = next(it) if has_norm else None
    j = pl.program_id(1)

    if has_norm:
        @pl.when(j == 0)
        def _():
            xf = x_ref[...].astype(F32)
            ms = jnp.mean(xf * xf, axis=-1, keepdims=True)
            xn_ref[...] = (xf * lax.rsqrt(ms + RMS_EPS) * g_ref[...]).astype(BF16)

        lhs_ref = xn_ref
    else:
        lhs_ref = x_ref

    if has_extra:
        @pl.when(j == 0)
        def _():
            oe_ref[...] = _dot_nt(lhs_ref[...].astype(BF16), we_ref[...].astype(BF16))

    wt = w_ref[...].astype(BF16)
    lhs = lhs_ref[...].astype(BF16)
    v = _dot_nt(lhs, wt) if w_rows_are_outputs else _dot(lhs, wt)
    if has_res:
        v = v + r_ref[...]
    o_ref[...] = v.astype(o_ref.dtype)


def _matmul(x, w, *, norm_g=None, residual=None, out_dtype=F32, tm=2048, tn=512,
            n_cols=None, w_rows_are_outputs=False, w_extra=None, name="matmul"):
    m, kdim = x.shape
    n_all = w.shape[0] if w_rows_are_outputs else w.shape[1]
    n = n_all if n_cols is None else n_cols
    tm = min(tm, m)
    tn = min(tn, n)
    has_norm = norm_g is not None
    assert m % tm == 0 and n % tn == 0
    grid = (m // tm, n // tn)

    x_mode = {"pipeline_mode": pl.Buffered(1)} if x.dtype == F32 else {}
    in_specs = [pl.BlockSpec((tm, kdim), lambda i, j: (i, 0), **x_mode)]
    args = [x]
    if has_norm:
        in_specs.append(pl.BlockSpec((1, kdim), lambda i, j: (0, 0)))
        args.append(norm_g.reshape(1, kdim).astype(F32))
    if w_rows_are_outputs:
        in_specs.append(pl.BlockSpec((tn, kdim), lambda i, j: (j, 0)))
    else:
        in_specs.append(pl.BlockSpec((kdim, tn), lambda i, j: (0, j)))
    args.append(w)
    if residual is not None:
        in_specs.append(pl.BlockSpec((tm, tn), lambda i, j: (i, j)))
        args.append(residual)
    out_specs = pl.BlockSpec((tm, tn), lambda i, j: (i, j))
    out_shape = jax.ShapeDtypeStruct((m, n), out_dtype)
    if w_extra is not None:
        in_specs.append(pl.BlockSpec((LANES, kdim), lambda i, j: (0, 0)))
        args.append(w_extra)
        out_specs = [out_specs, pl.BlockSpec((tm, LANES), lambda i, j: (i, 0))]
        out_shape = [out_shape, jax.ShapeDtypeStruct((m, LANES), F32)]

    return pl.pallas_call(
        functools.partial(_mm_kernel, has_norm=has_norm, has_res=residual is not None,
                          has_extra=w_extra is not None,
                          w_rows_are_outputs=w_rows_are_outputs),
        grid=grid,
        in_specs=in_specs,
        out_specs=out_specs,
        out_shape=out_shape,
        scratch_shapes=[pltpu.VMEM((tm, kdim), BF16)] if has_norm else [],
        compiler_params=_cparams(2),
        name=name,
    )(*args)


HALO = SUBLANES


def _causal_conv(buf, x_ref, cw_ref, cb_ref, n_taps):
    t_rows = x_ref.shape[0]
    buf[HALO:HALO + t_rows, :] = x_ref[...]
    ext = buf[...]
    acc = cw_ref[0:1, :] * ext
    for k in range(1, n_taps):
        acc = pltpu.roll(acc, 1, 0) + cw_ref[k:k + 1, :] * ext
    buf[0:HALO, :] = buf[t_rows:t_rows + HALO, :]
    return cb_ref[...] + acc[HALO:, :]


def _rg_kernel(gate_ref, x_ref, cw_ref, cb_ref, wa_ref, wx_ref, ba_ref, bx_ref,
               lam_ref, wo_ref, h_ref, o_ref, xbuf, a_scr, b_scr, hcar, y_prev):
    t = pl.program_id(0)
    t_rows, width = x_ref.shape
    n_blocks, bw, _ = wa_ref.shape

    @pl.when(t == 0)
    def _():
        xbuf[0:HALO, :] = jnp.zeros((HALO, width), F32)
        hcar[...] = jnp.zeros_like(hcar)
        y_prev[...] = jnp.zeros_like(y_prev)

    xc = _causal_conv(xbuf, x_ref, cw_ref, cb_ref, RG_CONV)
    xcb = xc.astype(BF16)
    sp = _softplus(-lam_ref[...])
    yp = y_prev[...]
    ow = o_ref.shape[1] // n_blocks
    for c in range(n_blocks):
        osl = slice(c * ow, (c + 1) * ow)
        o_ref[:, osl] = h_ref[:, osl] + _dot(yp, wo_ref[:, osl])
        sl = slice(c * bw, (c + 1) * bw)
        xs = xcb[:, sl]
        r = _sigmoid(_dot(xs, wa_ref[c]) + ba_ref[:, sl])
        i = _sigmoid(_dot(xs, wx_ref[c]) + bx_ref[:, sl])
        log_a = -RG_C * r * sp[:, sl]
        th = jnp.tanh(log_a)
        mult = jnp.sqrt(-2.0 * th / (1.0 - th))
        a_scr[:, sl] = jnp.exp(log_a)
        b_scr[:, sl] = mult * (i * xc[:, sl])

    def step(r, h):
        h = a_scr[pl.ds(r, 1), :] * h + b_scr[pl.ds(r, 1), :]
        b_scr[pl.ds(r, 1), :] = h
        return h

    hcar[...] = lax.fori_loop(0, t_rows, step, hcar[...], unroll=8)
    y_prev[...] = (b_scr[...] * jax.nn.gelu(gate_ref[...])).astype(y_prev.dtype)


def _rg_core_out(xg, h, conv_w, conv_b, wa, ba, wx, bx, lam, w_out, t_rows=256):
    s, two_w = xg.shape
    width = two_w // 2
    d = w_out.shape[1]
    t_rows = min(t_rows, s)
    nt = s // t_rows
    row = lambda v: v.reshape(1, width).astype(F32)
    full2 = lambda shape: pl.BlockSpec(shape, lambda t: (0, 0))
    full3 = lambda shape: pl.BlockSpec(shape, lambda t: (0, 0, 0))
    cur = lambda t: jnp.minimum(t, nt - 1)
    prev = lambda t: jnp.maximum(t - 1, 0)
    return pl.pallas_call(
        _rg_kernel,
        grid=(nt + 1,),
        in_specs=[
            pl.BlockSpec((t_rows, width), lambda t: (cur(t), 0)),
            pl.BlockSpec((t_rows, width), lambda t: (cur(t), 1)),
            full2((RG_CONV, width)), full2((1, width)),
            full3(wa.shape), full3(wx.shape),
            full2((1, width)), full2((1, width)), full2((1, width)),
            pl.BlockSpec((width, d), lambda t: (0, 0), pipeline_mode=pl.Buffered(1)),
            pl.BlockSpec((t_rows, d), lambda t: (prev(t), 0)),
        ],
        out_specs=pl.BlockSpec((t_rows, d), lambda t: (prev(t), 0)),
        out_shape=jax.ShapeDtypeStruct((s, d), F32),
        scratch_shapes=[
            pltpu.VMEM((t_rows + HALO, width), F32),
            pltpu.VMEM((t_rows, width), F32),
            pltpu.VMEM((t_rows, width), F32),
            pltpu.VMEM((1, width), F32),
            pltpu.VMEM((t_rows, width), BF16),
        ],
        compiler_params=_cparams(1),
        name="rg_core_out",
    )(xg, xg, conv_w.astype(F32), row(conv_b), wa.astype(BF16), wx.astype(BF16),
      row(ba), row(bx), row(lam), w_out.astype(BF16), h)


def _rglru_layer(h, mix_norm, w_in, conv_w, conv_b, w_a, b_a, w_x, b_x, lam, w_out):
    xg = _matmul(h, w_in, norm_g=mix_norm, out_dtype=F32, name="rg_in")
    return _rg_core_out(xg, h, conv_w, conv_b, w_a, b_a, w_x, b_x, lam, w_out)


SSD_TILE = 128
SSD_HPG = 8
SSD_GW = SSD_HPG * SSD_HEAD_DIM
SSD_GPS = 8


def _cumsum(v, axis):
    n = v.shape[axis]
    idx = lax.broadcasted_iota(jnp.int32, v.shape, axis)
    shift = 1
    while shift < n:
        v = v + jnp.where(idx >= shift, pltpu.roll(v, shift, axis), 0.0)
        shift *= 2
    return v


def _expand_heads(cols, base):
    rows = cols.shape[0]
    lane = lax.broadcasted_iota(jnp.int32, (rows, LANES), 1)
    parts = []
    for v in range(SSD_GW // LANES):
        lo = jnp.broadcast_to(cols[:, base + 2 * v:base + 2 * v + 1], (rows, LANES))
        hi = jnp.broadcast_to(cols[:, base + 2 * v + 1:base + 2 * v + 2], (rows, LANES))
        parts.append(jnp.where(lane < SSD_HEAD_DIM, lo, hi))
    return jnp.concatenate(parts, axis=1)


def _ssd_kernel(z_ref, x_ref, b_ref, c_ref, cwx_ref, cwb_ref, cwc_ref, cbx_ref,
                cbb_ref, cbc_ref, dtr_ref, dbr_ref, alr_ref, dsk_ref, ng_ref,
                wo_ref, h_ref, o_ref, xbuf, bbuf, cbuf, st, y_prev):
    t = pl.program_id(1)
    rows = x_ref.shape[0]
    hpg, gw, ns = SSD_HPG, SSD_GW, SSD_STATE

    @pl.when(t == 0)
    def _():
        xbuf[0:HALO, :] = jnp.zeros((HALO, xbuf.shape[1]), F32)
        bbuf[0:HALO, :] = jnp.zeros((HALO, bbuf.shape[1]), F32)
        cbuf[0:HALO, :] = jnp.zeros((HALO, cbuf.shape[1]), F32)
        st[...] = jnp.zeros_like(st)
        y_prev[...] = jnp.zeros_like(y_prev)

    def conv_silu(buf, ref, cw, cb):
        v = _causal_conv(buf, ref, cw, cb, SSD_CONV)
        return v * _sigmoid(v)

    xs_all = conv_silu(xbuf, x_ref, cwx_ref, cbx_ref)
    bm_all = conv_silu(bbuf, b_ref, cwb_ref, cbb_ref)
    cm_all = conv_silu(cbuf, c_ref, cwc_ref, cbc_ref)

    li = lax.broadcasted_iota(jnp.int32, (rows, rows), 0)
    si = lax.broadcasted_iota(jnp.int32, (rows, rows), 1)
    causal = li >= si
    eye = jnp.where(li == si, 1.0, 0.0)
    lane = lax.broadcasted_iota(jnp.int32, (rows, LANES), 1)

    groups = range(x_ref.shape[1] // gw)
    gsl = [slice(gi * gw, (gi + 1) * gw) for gi in groups]
    xs = [xs_all[:, gsl[gi]] for gi in groups]
    bmb = [bm_all[:, gi * ns:(gi + 1) * ns].astype(BF16) for gi in groups]
    cmb = [cm_all[:, gi * ns:(gi + 1) * ns].astype(BF16) for gi in groups]
    cb = [_dot_nt(cmb[gi], bmb[gi]) for gi in groups]
    stg = [st[:, gsl[gi]] for gi in groups]
    y_off = [_dot(cmb[gi], stg[gi].astype(BF16)) for gi in groups]

    csr, both = [], []
    for gi in groups:
        hs = slice(gi * hpg, (gi + 1) * hpg)
        dtr = _softplus(dtr_ref[hs, :] + dbr_ref[hs, :])
        csr.append(_cumsum(dtr * (-jnp.exp(alr_ref[hs, :])), 1))
        both.append(lax.dot_general(eye, jnp.concatenate([dtr, csr[gi]], axis=0),
                                    (((1,), (1,)), ((), ())),
                                    precision=lax.Precision.HIGHEST,
                                    preferred_element_type=F32))
    last = [both[gi][rows - 1:rows, :] for gi in groups]

    xdt = [xs[gi] * _expand_heads(both[gi], 0) for gi in groups]
    xdtb = [xdt[gi].astype(BF16) for gi in groups]
    xw = [(xdt[gi] * _expand_heads(jnp.exp(last[gi] - both[gi]), hpg)).astype(BF16)
          for gi in groups]
    st_new = [_dot_tn(bmb[gi], xw[gi]) for gi in groups]

    ys = [[] for _ in groups]
    for v in range(gw // LANES):
        pair = [[] for _ in groups]
        for j in (2 * v, 2 * v + 1):
            for gi in groups:
                seg = both[gi][:, hpg + j:hpg + j + 1] - csr[gi][j:j + 1, :]
                dec = jnp.exp(jnp.where(causal, seg, NEG_INF))
                pair[gi].append(_dot((cb[gi] * dec).astype(BF16),
                                     xdtb[gi][:, v * LANES:(v + 1) * LANES]))
        for gi in groups:
            ys[gi].append(jnp.where(lane < SSD_HEAD_DIM, pair[gi][0], pair[gi][1]))

    yp = y_prev[...]
    ow = o_ref.shape[1] // len(groups)
    for gi in groups:
        osl = slice(gi * ow, (gi + 1) * ow)
        o_ref[:, osl] = h_ref[:, osl] + _dot(yp, wo_ref[:, osl])
        y = jnp.concatenate(ys[gi], axis=1)
        y = y + y_off[gi] * _expand_heads(jnp.exp(both[gi]), hpg)
        y = y + xs[gi] * dsk_ref[:, gsl[gi]]
        z = z_ref[:, gsl[gi]]
        yz = y * (z * _sigmoid(z))
        yz = yz * lax.rsqrt(jnp.mean(yz * yz, axis=-1, keepdims=True) + RMS_EPS)
        y_prev[:, gsl[gi]] = (yz * ng_ref[:, gsl[gi]]).astype(y_prev.dtype)
        st[:, gsl[gi]] = stg[gi] * _expand_heads(jnp.exp(last[gi]), hpg) + st_new[gi]


def _ssd_core_out(zx, dt_raw, h, conv_w, conv_b, dt_bias, a_log, d_skip, norm_g, w_out):
    s = zx.shape[0]
    g = SSD_GROUPS
    gps = SSD_GPS
    assert gps == g
    inner = g * SSD_GW
    d = w_out.shape[1]
    rows = min(SSD_TILE, s)
    nt = s // rows
    cur = lambda t: jnp.minimum(t, nt - 1)
    prev = lambda t: jnp.maximum(t - 1, 0)
    xw, bw = gps * SSD_GW, gps * SSD_STATE
    nx = inner // xw
    nb = 2 * inner // bw
    nc = nb + g * SSD_STATE // bw
    cwb0 = inner // bw
    cwc0 = cwb0 + g * SSD_STATE // bw
    hps = gps * SSD_HPG

    dt_t = jnp.transpose(dt_raw)
    colv = lambda v: jnp.pad(v.astype(F32), (0, LANES - v.shape[0])).reshape(LANES, 1)
    conv_w = conv_w.astype(F32)
    conv_b = conv_b.reshape(1, -1).astype(F32)
    dsk = jnp.repeat(d_skip.astype(F32), SSD_HEAD_DIM).reshape(1, inner)
    ng = norm_g.reshape(1, inner).astype(F32)

    gs = lambda shape, fn: pl.BlockSpec(shape, fn)
    in_specs = [
        gs((rows, xw), lambda gi, t: (cur(t), gi)),
        gs((rows, xw), lambda gi, t: (cur(t), nx + gi)),
        gs((rows, bw), lambda gi, t: (cur(t), nb + gi)),
        gs((rows, bw), lambda gi, t: (cur(t), nc + gi)),
        gs((SSD_CONV, xw), lambda gi, t: (0, gi)),
        gs((SSD_CONV, bw), lambda gi, t: (0, cwb0 + gi)),
        gs((SSD_CONV, bw), lambda gi, t: (0, cwc0 + gi)),
        gs((1, xw), lambda gi, t: (0, gi)),
        gs((1, bw), lambda gi, t: (0, cwb0 + gi)),
        gs((1, bw), lambda gi, t: (0, cwc0 + gi)),
        gs((hps, rows), lambda gi, t: (gi, cur(t))),
        gs((hps, 1), lambda gi, t: (gi, 0)),
        gs((hps, 1), lambda gi, t: (gi, 0)),
        gs((1, xw), lambda gi, t: (0, gi)),
        gs((1, xw), lambda gi, t: (0, gi)),
        pl.BlockSpec((inner, d), lambda gi, t: (0, 0), pipeline_mode=pl.Buffered(1)),
        gs((rows, d), lambda gi, t: (prev(t), 0)),
    ]
    return pl.pallas_call(
        _ssd_kernel,
        grid=(g // gps, nt + 1),
        in_specs=in_specs,
        out_specs=pl.BlockSpec((rows, d), lambda gi, t: (prev(t), 0)),
        out_shape=jax.ShapeDtypeStruct((s, d), F32),
        scratch_shapes=[
            pltpu.VMEM((rows + HALO, xw), F32),
            pltpu.VMEM((rows + HALO, bw), F32),
            pltpu.VMEM((rows + HALO, bw), F32),
            pltpu.VMEM((SSD_STATE, xw), F32),
            pltpu.VMEM((rows, inner), BF16),
        ],
        compiler_params=_cparams(2),
        name="ssd_core_out",
    )(zx, zx, zx, zx, conv_w, conv_w, conv_w, conv_b, conv_b, conv_b,
      dt_t, colv(dt_bias), colv(a_log), dsk, ng, w_out.astype(BF16), h)


def _ssd_layer(h, mix_norm, w_in, conv_w, conv_b, dt_bias, a_log, d_skip, norm_g, w_out):
    heads = dt_bias.shape[0]
    n_zx = w_in.shape[1] - heads
    w_t = jnp.transpose(w_in)
    w_dt = jnp.pad(w_t[n_zx:], ((0, LANES - heads), (0, 0)))
    zx, dt_raw = _matmul(h, w_t, norm_g=mix_norm, out_dtype=F32, n_cols=n_zx,
                         w_rows_are_outputs=True, w_extra=w_dt, name="ssd_in")
    return _ssd_core_out(zx, dt_raw, h, conv_w, conv_b, dt_bias, a_log, d_skip, norm_g, w_out)


def _rope_kernel(qkv_ref, kw_ref, cos_ref, sin_ref, cosi_ref, sini_ref,
                 q_ref, k_ref, v_ref, qi_ref, ki_ref, w_ref, *, q_scale, w_scale):
    rows = qkv_ref.shape[0]
    lane = lax.broadcasted_iota(jnp.int32, (rows, LANES), 1)
    cos, sin = cos_ref[...], sin_ref[...]
    cosi, sini = cosi_ref[...], sini_ref[...]
    half_i = IDX_DIM // 2
    lower_i = (lane & (IDX_DIM - 1)) < half_i

    def rope_head(x):
        return x * cos + pltpu.roll(x, ATT_HEAD_DIM // 2, 1) * sin

    def rope_idx(x):
        rot = jnp.where(lower_i, pltpu.roll(x, LANES - half_i, 1), pltpu.roll(x, half_i, 1))
        return x * cosi + rot * sini

    nq = q_ref.shape[1] // LANES
    nkv = k_ref.shape[1] // LANES
    nqi = qi_ref.shape[1] // LANES
    for hh in range(nq):
        x = qkv_ref[:, hh * LANES:(hh + 1) * LANES]
        q_ref[:, hh * LANES:(hh + 1) * LANES] = (rope_head(x) * q_scale).astype(q_ref.dtype)
    off = nq
    for hh in range(nkv):
        x = qkv_ref[:, (off + hh) * LANES:(off + hh + 1) * LANES]
        k_ref[:, hh * LANES:(hh + 1) * LANES] = rope_head(x).astype(k_ref.dtype)
    off += nkv
    v_ref[...] = qkv_ref[:, off * LANES:(off + nkv) * LANES].astype(v_ref.dtype)
    off += nkv
    for hh in range(nqi):
        x = qkv_ref[:, (off + hh) * LANES:(off + hh + 1) * LANES]
        qi_ref[:, hh * LANES:(hh + 1) * LANES] = rope_idx(x).astype(qi_ref.dtype)
    kw = kw_ref[...]
    swapped = pltpu.roll(kw, LANES // 2, 1)
    ki_ref[...] = rope_idx(jnp.where(lane < IDX_DIM, kw, swapped)).astype(ki_ref.dtype)
    w_ref[...] = jnp.where(lane < IDX_HEADS, swapped * w_scale, 0.0)


def _rope_tables(seq):
    def table(dim):
        inv = ROPE_THETA ** (-jnp.arange(0, dim, 2, dtype=F32) / dim)
        ang = jnp.arange(seq, dtype=F32)[:, None] * inv[None, :]
        return jnp.cos(ang), jnp.sin(ang)

    c, s = table(ATT_HEAD_DIM)
    ci, si = table(IDX_DIM)
    return (jnp.concatenate([c, c], axis=1), jnp.concatenate([-s, s], axis=1),
            jnp.concatenate([ci, ci, ci, ci], axis=1), jnp.concatenate([-si, si, -si, si], axis=1))


def _dsa_rope(qkv, kw, t_rows=256):
    s = qkv.shape[0]
    t_rows = min(t_rows, s)
    dq = ATT_HEADS * ATT_HEAD_DIM
    dkv = ATT_KV_HEADS * ATT_HEAD_DIM
    dqi = IDX_HEADS * IDX_DIM
    cos, sin, cosi, sini = _rope_tables(s)
    tile = lambda width: pl.BlockSpec((t_rows, width), lambda t: (t, 0))
    return pl.pallas_call(
        functools.partial(_rope_kernel, q_scale=ATT_HEAD_DIM ** -0.5 * math.log2(math.e),
                          w_scale=IDX_HEADS ** -0.5 * IDX_DIM ** -0.5),
        grid=(s // t_rows,),
        in_specs=[tile(qkv.shape[1]), tile(LANES), tile(LANES), tile(LANES), tile(LANES), tile(LANES)],
        out_specs=[tile(dq), tile(dkv), tile(dkv), tile(dqi), tile(LANES), tile(LANES)],
        out_shape=[
            jax.ShapeDtypeStruct((s, dq), BF16),
            jax.ShapeDtypeStruct((s, dkv), BF16),
            jax.ShapeDtypeStruct((s, dkv), BF16),
            jax.ShapeDtypeStruct((s, dqi), BF16),
            jax.ShapeDtypeStruct((s, LANES), BF16),
            jax.ShapeDtypeStruct((s, LANES), F32),
        ],
        compiler_params=_cparams(1),
        name="dsa_rope",
    )(qkv, kw, cos, sin, cosi, sini)


DSA_TQ = 128
DSA_KB = 512


def _sortable_key(score):
    bits = lax.bitcast_convert_type(score, jnp.int32)
    return jnp.where(bits < 0, bits ^ jnp.int32(0x7FFFFFFF), bits)


def _key_hi(key):
    return jnp.right_shift(key, 16).astype(jnp.int16)


def _key_lo(key):
    return ((key & 0xFFFF) - 32768).astype(jnp.int16)


def _tile_lanes(v, width):
    return jnp.concatenate([v] * (width // v.shape[1]), axis=1)


def _dsa_kernel(q_ref, qi_ref, w_ref, k_ref, v_ref, ki_ref, o_ref,
                sc, hi16, lo16, qa, wb, qs, m_scr, acc_scr, *, top_k):
    i = pl.program_id(0)
    tq = q_ref.shape[0]
    kb_size = sc.shape[2]
    nkb = ((i + 1) * tq + kb_size - 1) // kb_size
    lane = lax.broadcasted_iota(jnp.int32, (tq, LANES), 1)
    chunk_shift = CHUNK.bit_length() - 1
    q_chunk = (i * tq + lax.broadcasted_iota(jnp.int32, (tq, kb_size), 0)) >> chunk_shift
    k_lane = lax.broadcasted_iota(jnp.int32, (tq, kb_size), 1)

    def admissible(kb):
        return ((kb * kb_size + k_lane) >> chunk_shift) <= q_chunk

    for h in range(IDX_HEADS):
        pair = qi_ref[:, (h // 2) * LANES:(h // 2 + 1) * LANES]
        keep = (lane < IDX_DIM) if h % 2 == 0 else (lane >= IDX_DIM)
        qa[h] = jnp.where(keep, pair, jnp.zeros_like(pair))
        wb[h] = jnp.broadcast_to(w_ref[:, h:h + 1], (tq, LANES))

    def score_block(kb, carry):
        kis = ki_ref[pl.ds(pl.multiple_of(kb * kb_size, kb_size), kb_size), :]
        score = jnp.zeros((tq, kb_size), F32)
        for h in range(IDX_HEADS):
            lg = _dot_nt(qa[h], kis)
            score = score + jnp.maximum(lg, 0.0) * _tile_lanes(wb[h], kb_size)
        key = jnp.where(admissible(kb), _sortable_key(score), jnp.int32(KEY_NEG_INF))
        sc[kb] = key
        hi16[kb] = _key_hi(key)
        lo16[kb] = _key_lo(key)
        return carry

    lax.fori_loop(0, nkb, score_block, 0)

    @pl.when(nkb % 2 == 1)
    def _():
        pad = jnp.full((tq, kb_size), KEY_NEG_INF, jnp.int32)
        hi16[nkb] = _key_hi(pad)
        lo16[nkb] = _key_lo(pad)

    n_pairs = (nkb + 1) // 2
    i16 = jnp.int16

    def count16(arr, cand, strict):
        cand_t = _tile_lanes(cand.astype(i16), kb_size)

        def count_pair(pi, acc):
            for u in range(2):
                blk = arr[2 * pi + u]
                hit = (blk > cand_t) if strict else (blk >= cand_t)
                one = jnp.where(hit, jnp.ones_like(blk), jnp.zeros_like(blk))
                for c in range(kb_size // LANES):
                    acc = acc + one[:, c * LANES:(c + 1) * LANES]
            return acc

        acc = lax.fori_loop(0, n_pairs, count_pair, jnp.zeros((tq, LANES), i16))
        cnt = jnp.sum(acc.astype(jnp.int32).astype(F32), axis=1, keepdims=True)
        return jnp.broadcast_to(cnt, (tq, LANES))

    def kth_largest16(arr, target):
        def bit_step(b, thr):
            cand = thr + jnp.left_shift(jnp.int32(1), 15 - b)
            return jnp.where(count16(arr, cand, False) >= target, cand, thr)

        return lax.fori_loop(0, 16, bit_step, jnp.full((tq, LANES), -32768, jnp.int32))

    k_f = jnp.full((tq, LANES), float(top_k), F32)
    thr_hi = kth_largest16(hi16, k_f)
    need = k_f - count16(hi16, thr_hi, True)
    thr_hi_t = _tile_lanes(thr_hi.astype(i16), kb_size)

    def tie_block(kb, carry):
        lo16[kb] = jnp.where(hi16[kb] == thr_hi_t, lo16[kb], jnp.full((tq, kb_size), -32768, i16))
        return carry

    lax.fori_loop(0, 2 * n_pairs, tie_block, 0)
    thr_lo = kth_largest16(lo16, need)
    thr = jnp.left_shift(thr_hi, 16) | ((thr_lo + 32768) & 0xFFFF)
    thr_t = _tile_lanes(thr, kb_size)

    n_above = count16(lo16, thr_lo, True)
    need_eq = need - n_above
    n_eq = count16(lo16, thr_lo, False) - n_above
    real_thr = thr > KEY_NEG_INF
    extra = jnp.where(jnp.logical_and(real_thr, n_eq > need_eq), 1.0, 0.0)
    has_extra_ties = jnp.max(extra) > 0.0

    @pl.when(jnp.logical_not(has_extra_ties))
    def _():
        def bias_block(kb, carry):
            sel = jnp.logical_and(sc[kb] >= thr_t, admissible(kb))
            sc[kb] = jnp.where(sel, jnp.int32(0), jnp.int32(NEG_INF_BITS))
            return carry

        lax.fori_loop(0, nkb, bias_block, 0)

    @pl.when(has_extra_ties)
    def _():
        ki_ = lax.broadcasted_iota(jnp.int32, (kb_size, kb_size), 0)
        kj_ = lax.broadcasted_iota(jnp.int32, (kb_size, kb_size), 1)
        before = jnp.where(ki_ < kj_, 1.0, 0.0).astype(BF16)
        limit = _tile_lanes(jnp.where(real_thr, need_eq, float(kb_size * sc.shape[0])), kb_size)

        def bias_block(kb, seen):
            key = sc[kb]
            adm = admissible(kb)
            eq = jnp.logical_and(key == thr_t, adm)
            eq_b = jnp.where(eq, 1.0, 0.0).astype(BF16)
            rank = _dot(eq_b, before) + _tile_lanes(seen, kb_size)
            keep = jnp.logical_or(jnp.logical_and(key > thr_t, adm),
                                  jnp.logical_and(eq, rank < limit))
            sc[kb] = jnp.where(keep, jnp.int32(0), jnp.int32(NEG_INF_BITS))
            total = _dot(eq_b, jnp.ones((kb_size, LANES), BF16))
            return seen + total

        lax.fori_loop(0, nkb, bias_block, jnp.zeros((tq, LANES), F32))

    for g in range(ATT_KV_HEADS):
        qs[g] = jnp.concatenate(
            [q_ref[:, (g * ATT_GROUP + a) * LANES:(g * ATT_GROUP + a + 1) * LANES]
             for a in range(ATT_GROUP)], axis=0)
    m_scr[...] = jnp.full(m_scr.shape, NEG_INF, F32)
    acc_scr[...] = jnp.zeros_like(acc_scr)
    ones = jnp.ones((kb_size, LANES), BF16)

    def attn_block(kb):
        start = pl.multiple_of(kb * kb_size, kb_size)
        bias = lax.bitcast_convert_type(sc[kb], F32)
        bias = jnp.concatenate([bias] * ATT_GROUP, axis=0)
        for g in range(ATT_KV_HEADS):
            kblk = k_ref[pl.ds(start, kb_size), g * LANES:(g + 1) * LANES]
            vblk = v_ref[pl.ds(start, kb_size), g * LANES:(g + 1) * LANES]
            s = _dot_nt(qs[g], kblk) + bias
            m_old = m_scr[g]
            m_new = jnp.maximum(m_old, jnp.max(s, axis=1, keepdims=True))
            m_safe = jnp.where(m_new == NEG_INF, 0.0, m_new)
            alpha = jnp.exp2(m_old - m_safe)
            p = jnp.exp2(s - _tile_lanes(m_safe, kb_size))
            pv = _dot(p.astype(BF16), jnp.concatenate([vblk, ones], axis=1))
            acc_scr[g] = _tile_lanes(alpha, 2 * LANES) * acc_scr[g] + pv
            m_scr[g] = m_new

    def attn_pair(pi, carry):
        attn_block(2 * pi)
        attn_block(2 * pi + 1)
        return carry

    lax.fori_loop(0, nkb // 2, attn_pair, 0)

    @pl.when(nkb % 2 == 1)
    def _():
        attn_block(nkb - 1)

    for g in range(ATT_KV_HEADS):
        out = acc_scr[g][:, :LANES] / acc_scr[g][:, LANES:]
        for a in range(ATT_GROUP):
            hh = g * ATT_GROUP + a
            o_ref[:, hh * LANES:(hh + 1) * LANES] = out[a * tq:(a + 1) * tq].astype(o_ref.dtype)


def _dsa_attend(q, k, v, qi, ki, w):
    s = q.shape[0]
    tq = min(DSA_TQ, s)
    kb = min(DSA_KB, s)
    top_k = min(TOPK_MAX, s // 4)
    resident = lambda width: pl.BlockSpec((s, width), lambda i: (0, 0),
                                          pipeline_mode=pl.Buffered(1))
    tile = lambda width: pl.BlockSpec((tq, width), lambda i: (i, 0))
    return pl.pallas_call(
        functools.partial(_dsa_kernel, top_k=top_k),
        grid=(s // tq,),
        in_specs=[tile(q.shape[1]), tile(qi.shape[1]), tile(LANES),
                  resident(k.shape[1]), resident(v.shape[1]), resident(LANES)],
        out_specs=tile(q.shape[1]),
        out_shape=jax.ShapeDtypeStruct(q.shape, BF16),
        scratch_shapes=[
            pltpu.VMEM((s // kb, tq, kb), jnp.int32),
            pltpu.VMEM((2 * pl.cdiv(s // kb, 2), tq, kb), jnp.int16),
            pltpu.VMEM((2 * pl.cdiv(s // kb, 2), tq, kb), jnp.int16),
            pltpu.VMEM((IDX_HEADS, tq, LANES), BF16),
            pltpu.VMEM((IDX_HEADS, tq, LANES), F32),
            pltpu.VMEM((ATT_KV_HEADS, ATT_GROUP * tq, LANES), BF16),
            pltpu.VMEM((ATT_KV_HEADS, ATT_GROUP * tq, LANES), F32),
            pltpu.VMEM((ATT_KV_HEADS, ATT_GROUP * tq, 2 * LANES), F32),
        ],
        compiler_params=_cparams(1),
        name="dsa_attend",
    )(q, qi, w, k, v, ki)


def _dsa_layer(h, mix_norm, w_in, w_out):
    n_main = (ATT_HEADS + 2 * ATT_KV_HEADS) * ATT_HEAD_DIM + IDX_HEADS * IDX_DIM
    n_rest = w_in.shape[1] - n_main
    w_t = jnp.transpose(w_in)
    w_rest = jnp.pad(w_t[n_main:], ((0, LANES - n_rest), (0, 0)))
    qkv, kw = _matmul(h, w_t, norm_g=mix_norm, out_dtype=F32, n_cols=n_main,
                      w_rows_are_outputs=True, w_extra=w_rest, name="dsa_in")
    q, k, v, qi, ki, w = _dsa_rope(qkv, kw)
    o = _dsa_attend(q, k, v, qi, ki, w)
    return _matmul(o, w_out, residual=h, out_dtype=F32, name="dsa_out")


def _mlp_kernel(x_ref, g_ref, wu_ref, wd_ref, *rest, has_out_norm):
    og_ref = rest[0] if has_out_norm else None
    o_ref, xn_ref = rest[-2:]
    f = pl.program_id(1)

    @pl.when(f == 0)
    def _():
        xf = x_ref[...]
        ms = jnp.mean(xf * xf, axis=-1, keepdims=True)
        xn_ref[...] = (xf * lax.rsqrt(ms + RMS_EPS) * g_ref[...]).astype(BF16)
        o_ref[...] = xf

    u = jnp.maximum(_dot(xn_ref[...], wu_ref[...].astype(BF16)), 0.0)
    o_ref[...] += _dot((u * u).astype(BF16), wd_ref[...].astype(BF16))

    if has_out_norm:
        @pl.when(f == pl.num_programs(1) - 1)
        def _():
            of = o_ref[...]
            ms = jnp.mean(of * of, axis=-1, keepdims=True)
            o_ref[...] = of * lax.rsqrt(ms + RMS_EPS) * og_ref[...]


def _mlp(h, norm_g, w_up, w_down, out_norm_g=None, tm=1024, tf=512):
    m, d = h.shape
    dff = w_up.shape[1]
    tm = min(tm, m)
    tf = min(tf, dff)
    row = lambda v: v.reshape(1, d).astype(F32)
    gain_spec = pl.BlockSpec((1, d), lambda i, f: (0, 0))
    in_specs = [
        pl.BlockSpec((tm, d), lambda i, f: (i, 0), pipeline_mode=pl.Buffered(1)),
        gain_spec,
        pl.BlockSpec((d, tf), lambda i, f: (0, f)),
        pl.BlockSpec((tf, d), lambda i, f: (f, 0)),
    ]
    args = [h, row(norm_g), w_up, w_down]
    if out_norm_g is not None:
        in_specs.append(gain_spec)
        args.append(row(out_norm_g))
    return pl.pallas_call(
        functools.partial(_mlp_kernel, has_out_norm=out_norm_g is not None),
        grid=(m // tm, dff // tf),
        in_specs=in_specs,
        out_specs=pl.BlockSpec((tm, d), lambda i, f: (i, 0)),
        out_shape=jax.ShapeDtypeStruct((m, d), F32),
        scratch_shapes=[pltpu.VMEM((tm, d), BF16)],
        compiler_params=_cparams(2),
        name="mlp",
    )(*args)


def kernel(x, l0_mix_norm, l0_rg_in_w, l0_rg_conv_w, l0_rg_conv_b, l0_rg_wa, l0_rg_ba, l0_rg_wx, l0_rg_bx, l0_rg_lambda, l0_rg_out_w, l0_mlp_norm, l0_mlp_up, l0_mlp_down, l1_mix_norm, l1_dsa_in_w, l1_dsa_out_w, l1_mlp_norm, l1_mlp_up, l1_mlp_down, l2_mix_norm, l2_ssd_in_w, l2_ssd_conv_w, l2_ssd_conv_b, l2_ssd_dt_bias, l2_ssd_a_log, l2_ssd_d, l2_ssd_norm, l2_ssd_out_w, l2_mlp_norm, l2_mlp_up, l2_mlp_down, l3_mix_norm, l3_rg_in_w, l3_rg_conv_w, l3_rg_conv_b, l3_rg_wa, l3_rg_ba, l3_rg_wx, l3_rg_bx, l3_rg_lambda, l3_rg_out_w, l3_mlp_norm, l3_mlp_up, l3_mlp_down, final_norm):
    bsz, seq, d = x.shape
    outs = []
    for b in range(bsz):
        h = x[b]
        h = _rglru_layer(h, l0_mix_norm, l0_rg_in_w, l0_rg_conv_w, l0_rg_conv_b, l0_rg_wa,
                         l0_rg_ba, l0_rg_wx, l0_rg_bx, l0_rg_lambda, l0_rg_out_w)
        h = _mlp(h, l0_mlp_norm, l0_mlp_up, l0_mlp_down)
        h = _dsa_layer(h, l1_mix_norm, l1_dsa_in_w, l1_dsa_out_w)
        h = _mlp(h, l1_mlp_norm, l1_mlp_up, l1_mlp_down)
        h = _ssd_layer(h, l2_mix_norm, l2_ssd_in_w, l2_ssd_conv_w, l2_ssd_conv_b, l2_ssd_dt_bias,
                       l2_ssd_a_log, l2_ssd_d, l2_ssd_norm, l2_ssd_out_w)
        h = _mlp(h, l2_mlp_norm, l2_mlp_up, l2_mlp_down)
        h = _rglru_layer(h, l3_mix_norm, l3_rg_in_w, l3_rg_conv_w, l3_rg_conv_b, l3_rg_wa,
                         l3_rg_ba, l3_rg_wx, l3_rg_bx, l3_rg_lambda, l3_rg_out_w)
        outs.append(_mlp(h, l3_mlp_norm, l3_mlp_up, l3_mlp_down, out_norm_g=final_norm))
    return jnp.stack(outs, axis=0)
```

```python
import functools
import math

import jax
import jax.numpy as jnp
from jax import lax
from jax.experimental import pallas as pl
from jax.experimental.pallas import tpu as pltpu

F32 = jnp.float32
BF16 = jnp.bfloat16

RMS_EPS = 1e-6
ROPE_THETA = 10000.0
CHUNK = 64
RG_BLOCKS = 8
RG_CONV = 4
RG_C = 8.0
ATT_HEADS = 16
ATT_KV_HEADS = 4
ATT_HEAD_DIM = 128
ATT_GROUP = ATT_HEADS // ATT_KV_HEADS
IDX_HEADS = 16
IDX_DIM = 64
TOPK_MAX = 256
SSD_HEAD_DIM = 64
SSD_GROUPS = 8
SSD_STATE = 128
SSD_CONV = 4

V7X_VMEM_BYTES = 64 * 1024 * 1024
VMEM_LIMIT_BYTES = V7X_VMEM_BYTES - 8 * 1024 * 1024
LANES = 128
SUBLANES = 8

NEG_INF = float("-inf")
KEY_NEG_INF = -2139095041
NEG_INF_BITS = -8388608


def _cparams(n_axes):
    return pltpu.CompilerParams(
        dimension_semantics=("arbitrary",) * n_axes,
        vmem_limit_bytes=VMEM_LIMIT_BYTES,
    )


def _dot(a, b):
    return jnp.dot(a, b, preferred_element_type=F32)


def _dot_nt(a, b):
    return lax.dot_general(a, b, (((1,), (1,)), ((), ())), preferred_element_type=F32)


def _dot_tn(a, b):
    return lax.dot_general(a, b, (((0,), (0,)), ((), ())), preferred_element_type=F32)


def _softplus(x):
    return jnp.maximum(x, 0.0) + jnp.log1p(jnp.exp(-jnp.abs(x)))


def _sigmoid(x):
    return jax.nn.sigmoid(x)


def _mm_kernel(*refs, has_norm, has_res, has_extra, w_rows_are_outputs):
    it = iter(refs)
    x_ref = next(it)
    g_ref = next(it) if has_norm else None
    w_ref = next(it)
    r_ref = next(it) if has_res else None
    we_ref = next(it) if has_extra else None
    o_ref = next(it)
    oe_ref = next(it) if has_extra else None
    xn_ref = next(it) if has_norm else None
    j = pl.program_id(1)

    if has_norm:
        @pl.when(j == 0)
        def _():
            xf = x_ref[...].astype(F32)
            ms = jnp.mean(xf * xf, axis=-1, keepdims=True)
            xn_ref[...] = (xf * lax.rsqrt(ms + RMS_EPS) * g_ref[...]).astype(BF16)

        lhs_ref = xn_ref
    else:
        lhs_ref = x_ref

    if has_extra:
        @pl.when(j == 0)
        def _():
            oe_ref[...] = _dot_nt(lhs_ref[...].astype(BF16), we_ref[...].astype(BF16))

    wt = w_ref[...].astype(BF16)
    lhs = lhs_ref[...].astype(BF16)
    v = _dot_nt(lhs, wt) if w_rows_are_outputs else _dot(lhs, wt)
    if has_res:
        v = v + r_ref[...]
    o_ref[...] = v.astype(o_ref.dtype)


def _matmul(x, w, *, norm_g=None, residual=None, out_dtype=F32, tm=2048, tn=512,
            n_cols=None, w_rows_are_outputs=False, w_extra=None, name="matmul"):
    m, kdim = x.shape
    n_all = w.shape[0] if w_rows_are_outputs else w.shape[1]
    n = n_all if n_cols is None else n_cols
    tm = min(tm, m)
    tn = min(tn, n)
    has_norm = norm_g is not None
    assert m % tm == 0 and n % tn == 0
    grid = (m // tm, n // tn)

    x_mode = {"pipeline_mode": pl.Buffered(1)} if x.dtype == F32 else {}
    in_specs = [pl.BlockSpec((tm, kdim), lambda i, j: (i, 0), **x_mode)]
    args = [x]
    if has_norm:
        in_specs.append(pl.BlockSpec((1, kdim), lambda i, j: (0, 0)))
        args.append(norm_g.reshape(1, kdim).astype(F32))
    if w_rows_are_outputs:
        in_specs.append(pl.BlockSpec((tn, kdim), lambda i, j: (j, 0)))
    else:
        in_specs.append(pl.BlockSpec((kdim, tn), lambda i, j: (0, j)))
    args.append(w)
    if residual is not None:
        in_specs.append(pl.BlockSpec((tm, tn), lambda i, j: (i, j)))
        args.append(residual)
    out_specs = pl.BlockSpec((tm, tn), lambda i, j: (i, j))
    out_shape = jax.ShapeDtypeStruct((m, n), out_dtype)
    if w_extra is not None:
        in_specs.append(pl.BlockSpec((LANES, kdim), lambda i, j: (0, 0)))
        args.append(w_extra)
        out_specs = [out_specs, pl.BlockSpec((tm, LANES), lambda i, j: (i, 0))]
        out_shape = [out_shape, jax.ShapeDtypeStruct((m, LANES), F32)]

    return pl.pallas_call(
        functools.partial(_mm_kernel, has_norm=has_norm, has_res=residual is not None,
                          has_extra=w_extra is not None,
                          w_rows_are_outputs=w_rows_are_outputs),
        grid=grid,
        in_specs=in_specs,
        out_specs=out_specs,
        out_shape=out_shape,
        scratch_shapes=[pltpu.VMEM((tm, kdim), BF16)] if has_norm else [],
        compiler_params=_cparams(2),
        name=name,
    )(*args)


HALO = SUBLANES


def _causal_conv(buf, x_ref, cw_ref, cb_ref, n_taps):
    t_rows = x_ref.shape[0]
    buf[HALO:HALO + t_rows, :] = x_ref[...]
    ext = buf[...]
    acc = cw_ref[0:1, :] * ext
    for k in range(1, n_taps):
        acc = pltpu.roll(acc, 1, 0) + cw_ref[k:k + 1, :] * ext
    buf[0:HALO, :] = buf[t_rows:t_rows + HALO, :]
    return cb_ref[...] + acc[HALO:, :]


def _rg_kernel(gate_ref, x_ref, cw_ref, cb_ref, wa_ref, wx_ref, ba_ref, bx_ref,
               lam_ref, wo_ref, h_ref, o_ref, xbuf, a_scr, b_scr, hcar, y_prev):
    t = pl.program_id(0)
    t_rows, width = x_ref.shape
    n_blocks, bw, _ = wa_ref.shape

    @pl.when(t == 0)
    def _():
        xbuf[0:HALO, :] = jnp.zeros((HALO, width), F32)
        hcar[...] = jnp.zeros_like(hcar)
        y_prev[...] = jnp.zeros_like(y_prev)

    xc = _causal_conv(xbuf, x_ref, cw_ref, cb_ref, RG_CONV)
    xcb = xc.astype(BF16)
    sp = _softplus(-lam_ref[...])
    yp = y_prev[...]
    ow = o_ref.shape[1] // n_blocks
    for c in range(n_blocks):
        osl = slice(c * ow, (c + 1) * ow)
        o_ref[:, osl] = h_ref[:, osl] + _dot(yp, wo_ref[:, osl])
        sl = slice(c * bw, (c + 1) * bw)
        xs = xcb[:, sl]
        r = _sigmoid(_dot(xs, wa_ref[c]) + ba_ref[:, sl])
        i = _sigmoid(_dot(xs, wx_ref[c]) + bx_ref[:, sl])
        log_a = -RG_C * r * sp[:, sl]
        th = jnp.tanh(log_a)
        mult = jnp.sqrt(-2.0 * th / (1.0 - th))
        a_scr[:, sl] = jnp.exp(log_a)
        b_scr[:, sl] = mult * (i * xc[:, sl])

    def step(r, h):
        h = a_scr[pl.ds(r, 1), :] * h + b_scr[pl.ds(r, 1), :]
        b_scr[pl.ds(r, 1), :] = h
        return h

    hcar[...] = lax.fori_loop(0, t_rows, step, hcar[...], unroll=8)
    y_prev[...] = (b_scr[...] * jax.nn.gelu(gate_ref[...])).astype(y_prev.dtype)


def _rg_core_out(xg, h, conv_w, conv_b, wa, ba, wx, bx, lam, w_out, t_rows=256):
    s, two_w = xg.shape
    width = two_w // 2
    d = w_out.shape[1]
    t_rows = min(t_rows, s)
    nt = s // t_rows
    row = lambda v: v.reshape(1, width).astype(F32)
    full2 = lambda shape: pl.BlockSpec(shape, lambda t: (0, 0))
    full3 = lambda shape: pl.BlockSpec(shape, lambda t: (0, 0, 0))
    cur = lambda t: jnp.minimum(t, nt - 1)
    prev = lambda t: jnp.maximum(t - 1, 0)
    return pl.pallas_call(
        _rg_kernel,
        grid=(nt + 1,),
        in_specs=[
            pl.BlockSpec((t_rows, width), lambda t: (cur(t), 0)),
            pl.BlockSpec((t_rows, width), lambda t: (cur(t), 1)),
            full2((RG_CONV, width)), full2((1, width)),
            full3(wa.shape), full3(wx.shape),
            full2((1, width)), full2((1, width)), full2((1, width)),
            pl.BlockSpec((width, d), lambda t: (0, 0), pipeline_mode=pl.Buffered(1)),
            pl.BlockSpec((t_rows, d), lambda t: (prev(t), 0)),
        ],
        out_specs=pl.BlockSpec((t_rows, d), lambda t: (prev(t), 0)),
        out_shape=jax.ShapeDtypeStruct((s, d), F32),
        scratch_shapes=[
            pltpu.VMEM((t_rows + HALO, width), F32),
            pltpu.VMEM((t_rows, width), F32),
            pltpu.VMEM((t_rows, width), F32),
            pltpu.VMEM((1, width), F32),
            pltpu.VMEM((t_rows, width), BF16),
        ],
        compiler_params=_cparams(1),
        name="rg_core_out",
    )(xg, xg, conv_w.astype(F32), row(conv_b), wa.astype(BF16), wx.astype(BF16),
      row(ba), row(bx), row(lam), w_out.astype(BF16), h)


def _rglru_layer(h, mix_norm, w_in, conv_w, conv_b, w_a, b_a, w_x, b_x, lam, w_out):
    xg = _matmul(h, w_in, norm_g=mix_norm, out_dtype=F32, name="rg_in")
    return _rg_core_out(xg, h, conv_w, conv_b, w_a, b_a, w_x, b_x, lam, w_out)


SSD_TILE = 128
SSD_HPG = 8
SSD_GW = SSD_HPG * SSD_HEAD_DIM
SSD_GPS = 8


def _cumsum(v, axis):
    n = v.shape[axis]
    idx = lax.broadcasted_iota(jnp.int32, v.shape, axis)
    shift = 1
    while shift < n:
        v = v + jnp.where(idx >= shift, pltpu.roll(v, shift, axis), 0.0)
        shift *= 2
    return v


def _expand_heads(cols, base):
    rows = cols.shape[0]
    lane = lax.broadcasted_iota(jnp.int32, (rows, LANES), 1)
    parts = []
    for v in range(SSD_GW // LANES):
        lo = jnp.broadcast_to(cols[:, base + 2 * v:base + 2 * v + 1], (rows, LANES))
        hi = jnp.broadcast_to(cols[:, base + 2 * v + 1:base + 2 * v + 2], (rows, LANES))
        parts.append(jnp.where(lane < SSD_HEAD_DIM, lo, hi))
    return jnp.concatenate(parts, axis=1)


def _ssd_kernel(z_ref, x_ref, b_ref, c_ref, cwx_ref, cwb_ref, cwc_ref, cbx_ref,
                cbb_ref, cbc_ref, dtr_ref, dbr_ref, alr_ref, dsk_ref, ng_ref,
                wo_ref, h_ref, o_ref, xbuf, bbuf, cbuf, st, y_prev):
    t = pl.program_id(1)
    rows = x_ref.shape[0]
    hpg, gw, ns = SSD_HPG, SSD_GW, SSD_STATE

    @pl.when(t == 0)
    def _():
        xbuf[0:HALO, :] = jnp.zeros((HALO, xbuf.shape[1]), F32)
        bbuf[0:HALO, :] = jnp.zeros((HALO, bbuf.shape[1]), F32)
        cbuf[0:HALO, :] = jnp.zeros((HALO, cbuf.shape[1]), F32)
        st[...] = jnp.zeros_like(st)
        y_prev[...] = jnp.zeros_like(y_prev)

    def conv_silu(buf, ref, cw, cb):
        v = _causal_conv(buf, ref, cw, cb, SSD_CONV)
        return v * _sigmoid(v)

    xs_all = conv_silu(xbuf, x_ref, cwx_ref, cbx_ref)
    bm_all = conv_silu(bbuf, b_ref, cwb_ref, cbb_ref)
    cm_all = conv_silu(cbuf, c_ref, cwc_ref, cbc_ref)

    li = lax.broadcasted_iota(jnp.int32, (rows, rows), 0)
    si = lax.broadcasted_iota(jnp.int32, (rows, rows), 1)
    causal = li >= si
    eye = jnp.where(li == si, 1.0, 0.0)
    lane = lax.broadcasted_iota(jnp.int32, (rows, LANES), 1)

    groups = range(x_ref.shape[1] // gw)
    gsl = [slice(gi * gw, (gi + 1) * gw) for gi in groups]
    xs = [xs_all[:, gsl[gi]] for gi in groups]
    bmb = [bm_all[:, gi * ns:(gi + 1) * ns].astype(BF16) for gi in groups]
    cmb = [cm_all[:, gi * ns:(gi + 1) * ns].astype(BF16) for gi in groups]
    cb = [_dot_nt(cmb[gi], bmb[gi]) for gi in groups]
    stg = [st[:, gsl[gi]] for gi in groups]
    y_off = [_dot(cmb[gi], stg[gi].astype(BF16)) for gi in groups]

    csr, both = [], []
    for gi in groups:
        hs = slice(gi * hpg, (gi + 1) * hpg)
        dtr = _softplus(dtr_ref[hs, :] + dbr_ref[hs, :])
        csr.append(_cumsum(dtr * (-jnp.exp(alr_ref[hs, :])), 1))
        both.append(lax.dot_general(eye, jnp.concatenate([dtr, csr[gi]], axis=0),
                                    (((1,), (1,)), ((), ())),
                                    precision=lax.Precision.HIGHEST,
                                    preferred_element_type=F32))
    last = [both[gi][rows - 1:rows, :] for gi in groups]

    xdt = [xs[gi] * _expand_heads(both[gi], 0) for gi in groups]
    xdtb = [xdt[gi].astype(BF16) for gi in groups]
    xw = [(xdt[gi] * _expand_heads(jnp.exp(last[gi] - both[gi]), hpg)).astype(BF16)
          for gi in groups]
    st_new = [_dot_tn(bmb[gi], xw[gi]) for gi in groups]

    ys = [[] for _ in groups]
    for v in range(gw // LANES):
        pair = [[] for _ in groups]
        for j in (2 * v, 2 * v + 1):
            for gi in groups:
                seg = both[gi][:, hpg + j:hpg + j + 1] - csr[gi][j:j + 1, :]
                dec = jnp.exp(jnp.where(causal, seg, NEG_INF))
                pair[gi].append(_dot((cb[gi] * dec).astype(BF16),
                                     xdtb[gi][:, v * LANES:(v + 1) * LANES]))
        for gi in groups:
            ys[gi].append(jnp.where(lane < SSD_HEAD_DIM, pair[gi][0], pair[gi][1]))

    yp = y_prev[...]
    ow = o_ref.shape[1] // len(groups)
    for gi in groups:
        osl = slice(gi * ow, (gi + 1) * ow)
        o_ref[:, osl] = h_ref[:, osl] + _dot(yp, wo_ref[:, osl])
        y = jnp.concatenate(ys[gi], axis=1)
        y = y + y_off[gi] * _expand_heads(jnp.exp(both[gi]), hpg)
        y = y + xs[gi] * dsk_ref[:, gsl[gi]]
        z = z_ref[:, gsl[gi]]
        yz = y * (z * _sigmoid(z))
        yz = yz * lax.rsqrt(jnp.mean(yz * yz, axis=-1, keepdims=True) + RMS_EPS)
        y_prev[:, gsl[gi]] = (yz * ng_ref[:, gsl[gi]]).astype(y_prev.dtype)
        st[:, gsl[gi]] = stg[gi] * _expand_heads(jnp.exp(last[gi]), hpg) + st_new[gi]


def _ssd_core_out(zx, dt_raw, h, conv_w, conv_b, dt_bias, a_log, d_skip, norm_g, w_out):
    s = zx.shape[0]
    g = SSD_GROUPS
    gps = SSD_GPS
    assert gps == g
    inner = g * SSD_GW
    d = w_out.shape[1]
    rows = min(SSD_TILE, s)
    nt = s // rows
    cur = lambda t: jnp.minimum(t, nt - 1)
    prev = lambda t: jnp.maximum(t - 1, 0)
    xw, bw = gps * SSD_GW, gps * SSD_STATE
    nx = inner // xw
    nb = 2 * inner // bw
    nc = nb + g * SSD_STATE // bw
    cwb0 = inner // bw
    cwc0 = cwb0 + g * SSD_STATE // bw
    hps = gps * SSD_HPG

    dt_t = jnp.transpose(dt_raw)
    colv = lambda v: jnp.pad(v.astype(F32), (0, LANES - v.shape[0])).reshape(LANES, 1)
    conv_w = conv_w.astype(F32)
    conv_b = conv_b.reshape(1, -1).astype(F32)
    dsk = jnp.repeat(d_skip.astype(F32), SSD_HEAD_DIM).reshape(1, inner)
    ng = norm_g.reshape(1, inner).astype(F32)

    gs = lambda shape, fn: pl.BlockSpec(shape, fn)
    in_specs = [
        gs((rows, xw), lambda gi, t: (cur(t), gi)),
        gs((rows, xw), lambda gi, t: (cur(t), nx + gi)),
        gs((rows, bw), lambda gi, t: (cur(t), nb + gi)),
        gs((rows, bw), lambda gi, t: (cur(t), nc + gi)),
        gs((SSD_CONV, xw), lambda gi, t: (0, gi)),
        gs((SSD_CONV, bw), lambda gi, t: (0, cwb0 + gi)),
        gs((SSD_CONV, bw), lambda gi, t: (0, cwc0 + gi)),
        gs((1, xw), lambda gi, t: (0, gi)),
        gs((1, bw), lambda gi, t: (0, cwb0 + gi)),
        gs((1, bw), lambda gi, t: (0, cwc0 + gi)),
        gs((hps, rows), lambda gi, t: (gi, cur(t))),
        gs((hps, 1), lambda gi, t: (gi, 0)),
        gs((hps, 1), lambda gi, t: (gi, 0)),
        gs((1, xw), lambda gi, t: (0, gi)),
        gs((1, xw), lambda gi, t: (0, gi)),
        pl.BlockSpec((inner, d), lambda gi, t: (0, 0), pipeline_mode=pl.Buffered(1)),
        gs((rows, d), lambda gi, t: (prev(t), 0)),
    ]
    return pl.pallas_call(
        _ssd_kernel,
        grid=(g // gps, nt + 1),
        in_specs=in_specs,
        out_specs=pl.BlockSpec((rows, d), lambda gi, t: (prev(t), 0)),
        out_shape=jax.ShapeDtypeStruct((s, d), F32),
        scratch_shapes=[
            pltpu.VMEM((rows + HALO, xw), F32),
            pltpu.VMEM((rows + HALO, bw), F32),
            pltpu.VMEM((rows + HALO, bw), F32),
            pltpu.VMEM((SSD_STATE, xw), F32),
            pltpu.VMEM((rows, inner), BF16),
        ],
        compiler_params=_cparams(2),
        name="ssd_core_out",
    )(zx, zx, zx, zx, conv_w, conv_w, conv_w, conv_b, conv_b, conv_b,
      dt_t, colv(dt_bias), colv(a_log), dsk, ng, w_out.astype(BF16), h)


def _ssd_layer(h, mix_norm, w_in, conv_w, conv_b, dt_bias, a_log, d_skip, norm_g, w_out):
    heads = dt_bias.shape[0]
    n_zx = w_in.shape[1] - heads
    w_t = jnp.transpose(w_in)
    w_dt = jnp.pad(w_t[n_zx:], ((0, LANES - heads), (0, 0)))
    zx, dt_raw = _matmul(h, w_t, norm_g=mix_norm, out_dtype=F32, n_cols=n_zx,
                         w_rows_are_outputs=True, w_extra=w_dt, name="ssd_in")
    return _ssd_core_out(zx, dt_raw, h, conv_w, conv_b, dt_bias, a_log, d_skip, norm_g, w_out)


def _rope_kernel(qkv_ref, kw_ref, cos_ref, sin_ref, cosi_ref, sini_ref,
                 q_ref, k_ref, v_ref, qi_ref, ki_ref, w_ref, *, q_scale, w_scale):
    rows = qkv_ref.shape[0]
    lane = lax.broadcasted_iota(jnp.int32, (rows, LANES), 1)
    cos, sin = cos_ref[...], sin_ref[...]
    cosi, sini = cosi_ref[...], sini_ref[...]
    half_i = IDX_DIM // 2
    lower_i = (lane & (IDX_DIM - 1)) < half_i

    def rope_head(x):
        return x * cos + pltpu.roll(x, ATT_HEAD_DIM // 2, 1) * sin

    def rope_idx(x):
        rot = jnp.where(lower_i, pltpu.roll(x, LANES - half_i, 1), pltpu.roll(x, half_i, 1))
        return x * cosi + rot * sini

    nq = q_ref.shape[1] // LANES
    nkv = k_ref.shape[1] // LANES
    nqi = qi_ref.shape[1] // LANES
    for hh in range(nq):
        x = qkv_ref[:, hh * LANES:(hh + 1) * LANES]
        q_ref[:, hh * LANES:(hh + 1) * LANES] = (rope_head(x) * q_scale).astype(q_ref.dtype)
    off = nq
    for hh in range(nkv):
        x = qkv_ref[:, (off + hh) * LANES:(off + hh + 1) * LANES]
        k_ref[:, hh * LANES:(hh + 1) * LANES] = rope_head(x).astype(k_ref.dtype)
    off += nkv
    v_ref[...] = qkv_ref[:, off * LANES:(off + nkv) * LANES].astype(v_ref.dtype)
    off += nkv
    for hh in range(nqi):
        x = qkv_ref[:, (off + hh) * LANES:(off + hh + 1) * LANES]
        qi_ref[:, hh * LANES:(hh + 1) * LANES] = rope_idx(x).astype(qi_ref.dtype)
    kw = kw_ref[...]
    swapped = pltpu.roll(kw, LANES // 2, 1)
    ki_ref[...] = rope_idx(jnp.where(lane < IDX_DIM, kw, swapped)).astype(ki_ref.dtype)
    w_ref[...] = jnp.where(lane < IDX_HEADS, swapped * w_scale, 0.0)


def _rope_tables(seq):
    def table(dim):
        inv = ROPE_THETA ** (-jnp.arange(0, dim, 2, dtype=F32) / dim)
        ang = jnp.arange(seq, dtype=F32)[:, None] * inv[None, :]
        return jnp.cos(ang), jnp.sin(ang)

    c, s = table(ATT_HEAD_DIM)
    ci, si = table(IDX_DIM)
    return (jnp.concatenate([c, c], axis=1), jnp.concatenate([-s, s], axis=1),
            jnp.concatenate([ci, ci, ci, ci], axis=1), jnp.concatenate([-si, si, -si, si], axis=1))


def _dsa_rope(qkv, kw, t_rows=256):
    s = qkv.shape[0]
    t_rows = min(t_rows, s)
    dq = ATT_HEADS * ATT_HEAD_DIM
    dkv = ATT_KV_HEADS * ATT_HEAD_DIM
    dqi = IDX_HEADS * IDX_DIM
    cos, sin, cosi, sini = _rope_tables(s)
    tile = lambda width: pl.BlockSpec((t_rows, width), lambda t: (t, 0))
    return pl.pallas_call(
        functools.partial(_rope_kernel, q_scale=ATT_HEAD_DIM ** -0.5 * math.log2(math.e),
                          w_scale=IDX_HEADS ** -0.5 * IDX_DIM ** -0.5),
        grid=(s // t_rows,),
        in_specs=[tile(qkv.shape[1]), tile(LANES), tile(LANES), tile(LANES), tile(LANES), tile(LANES)],
        out_specs=[tile(dq), tile(dkv), tile(dkv), tile(dqi), tile(LANES), tile(LANES)],
        out_shape=[
            jax.ShapeDtypeStruct((s, dq), BF16),
            jax.ShapeDtypeStruct((s, dkv), BF16),
            jax.ShapeDtypeStruct((s, dkv), BF16),
            jax.ShapeDtypeStruct((s, dqi), BF16),
            jax.ShapeDtypeStruct((s, LANES), BF16),
            jax.ShapeDtypeStruct((s, LANES), F32),
        ],
        compiler_params=_cparams(1),
        name="dsa_rope",
    )(qkv, kw, cos, sin, cosi, sini)


DSA_TQ = 128
DSA_KB = 512


def _sortable_key(score):
    bits = lax.bitcast_convert_type(score, jnp.int32)
    return jnp.where(bits < 0, bits ^ jnp.int32(0x7FFFFFFF), bits)


def _key_hi(key):
    return jnp.right_shift(key, 16).astype(jnp.int16)


def _key_lo(key):
    return ((key & 0xFFFF) - 32768).astype(jnp.int16)


def _tile_lanes(v, width):
    return jnp.concatenate([v] * (width // v.shape[1]), axis=1)


def _dsa_kernel(q_ref, qi_ref, w_ref, k_ref, v_ref, ki_ref, o_ref,
                sc, hi16, lo16, qa, wb, qs, m_scr, acc_scr, *, top_k):
    i = pl.program_id(0)
    tq = q_ref.shape[0]
    kb_size = sc.shape[2]
    nkb = ((i + 1) * tq + kb_size - 1) // kb_size
    lane = lax.broadcasted_iota(jnp.int32, (tq, LANES), 1)
    chunk_shift = CHUNK.bit_length() - 1
    q_chunk = (i * tq + lax.broadcasted_iota(jnp.int32, (tq, kb_size), 0)) >> chunk_shift
    k_lane = lax.broadcasted_iota(jnp.int32, (tq, kb_size), 1)

    def admissible(kb):
        return ((kb * kb_size + k_lane) >> chunk_shift) <= q_chunk

    for h in range(IDX_HEADS):
        pair = qi_ref[:, (h // 2) * LANES:(h // 2 + 1) * LANES]
        keep = (lane < IDX_DIM) if h % 2 == 0 else (lane >= IDX_DIM)
        qa[h] = jnp.where(keep, pair, jnp.zeros_like(pair))
        wb[h] = jnp.broadcast_to(w_ref[:, h:h + 1], (tq, LANES))

    def score_block(kb, carry):
        kis = ki_ref[pl.ds(pl.multiple_of(kb * kb_size, kb_size), kb_size), :]
        score = jnp.zeros((tq, kb_size), F32)
        for h in range(IDX_HEADS):
            lg = _dot_nt(qa[h], kis)
            score = score + jnp.maximum(lg, 0.0) * _tile_lanes(wb[h], kb_size)
        key = jnp.where(admissible(kb), _sortable_key(score), jnp.int32(KEY_NEG_INF))
        sc[kb] = key
        hi16[kb] = _key_hi(key)
        lo16[kb] = _key_lo(key)
        return carry

    lax.fori_loop(0, nkb, score_block, 0)

    @pl.when(nkb % 2 == 1)
    def _():
        pad = jnp.full((tq, kb_size), KEY_NEG_INF, jnp.int32)
        hi16[nkb] = _key_hi(pad)
        lo16[nkb] = _key_lo(pad)

    n_pairs = (nkb + 1) // 2
    i16 = jnp.int16

    def count16(arr, cand, strict):
        cand_t = _tile_lanes(cand.astype(i16), kb_size)

        def count_pair(pi, acc):
            for u in range(2):
                blk = arr[2 * pi + u]
                hit = (blk > cand_t) if strict else (blk >= cand_t)
                one = jnp.where(hit, jnp.ones_like(blk), jnp.zeros_like(blk))
                for c in range(kb_size // LANES):
                    acc = acc + one[:, c * LANES:(c + 1) * LANES]
            return acc

        acc = lax.fori_loop(0, n_pairs, count_pair, jnp.zeros((tq, LANES), i16))
        cnt = jnp.sum(acc.astype(jnp.int32).astype(F32), axis=1, keepdims=True)
        return jnp.broadcast_to(cnt, (tq, LANES))

    def kth_largest16(arr, target):
        def bit_step(b, thr):
            cand = thr + jnp.left_shift(jnp.int32(1), 15 - b)
            return jnp.where(count16(arr, cand, False) >= target, cand, thr)

        return lax.fori_loop(0, 16, bit_step, jnp.full((tq, LANES), -32768, jnp.int32))

    k_f = jnp.full((tq, LANES), float(top_k), F32)
    thr_hi = kth_largest16(hi16, k_f)
    need = k_f - count16(hi16, thr_hi, True)
    thr_hi_t = _tile_lanes(thr_hi.astype(i16), kb_size)

    def tie_block(kb, carry):
        lo16[kb] = jnp.where(hi16[kb] == thr_hi_t, lo16[kb], jnp.full((tq, kb_size), -32768, i16))
        return carry

    lax.fori_loop(0, 2 * n_pairs, tie_block, 0)
    thr_lo = kth_largest16(lo16, need)
    thr = jnp.left_shift(thr_hi, 16) | ((thr_lo + 32768) & 0xFFFF)
    thr_t = _tile_lanes(thr, kb_size)

    n_above = count16(lo16, thr_lo, True)
    need_eq = need - n_above
    n_eq = count16(lo16, thr_lo, False) - n_above
    real_thr = thr > KEY_NEG_INF
    extra = jnp.where(jnp.logical_and(real_thr, n_eq > need_eq), 1.0, 0.0)
    has_extra_ties = jnp.max(extra) > 0.0

    @pl.when(jnp.logical_not(has_extra_ties))
    def _():
        def bias_block(kb, carry):
            sel = jnp.logical_and(sc[kb] >= thr_t, admissible(kb))
            sc[kb] = jnp.where(sel, jnp.int32(0), jnp.int32(NEG_INF_BITS))
            return carry

        lax.fori_loop(0, nkb, bias_block, 0)

    @pl.when(has_extra_ties)
    def _():
        ki_ = lax.broadcasted_iota(jnp.int32, (kb_size, kb_size), 0)
        kj_ = lax.broadcasted_iota(jnp.int32, (kb_size, kb_size), 1)
        before = jnp.where(ki_ < kj_, 1.0, 0.0).astype(BF16)
        limit = _tile_lanes(jnp.where(real_thr, need_eq, float(kb_size * sc.shape[0])), kb_size)

        def bias_block(kb, seen):
            key = sc[kb]
            adm = admissible(kb)
            eq = jnp.logical_and(key == thr_t, adm)
            eq_b = jnp.where(eq, 1.0, 0.0).astype(BF16)
            rank = _dot(eq_b, before) + _tile_lanes(seen, kb_size)
            keep = jnp.logical_or(jnp.logical_and(key > thr_t, adm),
                                  jnp.logical_and(eq, rank < limit))
            sc[kb] = jnp.where(keep, jnp.int32(0), jnp.int32(NEG_INF_BITS))
            total = _dot(eq_b, jnp.ones((kb_size, LANES), BF16))
            return seen + total

        lax.fori_loop(0, nkb, bias_block, jnp.zeros((tq, LANES), F32))

    for g in range(ATT_KV_HEADS):
        qs[g] = jnp.concatenate(
            [q_ref[:, (g * ATT_GROUP + a) * LANES:(g * ATT_GROUP + a + 1) * LANES]
             for a in range(ATT_GROUP)], axis=0)
    m_scr[...] = jnp.full(m_scr.shape, NEG_INF, F32)
    acc_scr[...] = jnp.zeros_like(acc_scr)
    ones = jnp.ones((kb_size, LANES), BF16)

    def attn_block(kb):
        start = pl.multiple_of(kb * kb_size, kb_size)
        bias = lax.bitcast_convert_type(sc[kb], F32)
        bias = jnp.concatenate([bias] * ATT_GROUP, axis=0)
        for g in range(ATT_KV_HEADS):
            kblk = k_ref[pl.ds(start, kb_size), g * LANES:(g + 1) * LANES]
            vblk = v_ref[pl.ds(start, kb_size), g * LANES:(g + 1) * LANES]
            s = _dot_nt(qs[g], kblk) + bias
            m_old = m_scr[g]
            m_new = jnp.maximum(m_old, jnp.max(s, axis=1, keepdims=True))
            m_safe = jnp.where(m_new == NEG_INF, 0.0, m_new)
            alpha = jnp.exp2(m_old - m_safe)
            p = jnp.exp2(s - _tile_lanes(m_safe, kb_size))
            pv = _dot(p.astype(BF16), jnp.concatenate([vblk, ones], axis=1))
            acc_scr[g] = _tile_lanes(alpha, 2 * LANES) * acc_scr[g] + pv
            m_scr[g] = m_new

    def attn_quad(qi, carry):
        for u in range(4):
            attn_block(4 * qi + u)
        return carry

    lax.fori_loop(0, nkb // 4, attn_quad, 0)

    @pl.when(nkb % 4 >= 2)
    def _():
        first = (nkb // 4) * 4
        attn_block(first)
        attn_block(first + 1)

    @pl.when(nkb % 2 == 1)
    def _():
        attn_block(nkb - 1)

    for g in range(ATT_KV_HEADS):
        out = acc_scr[g][:, :LANES] / acc_scr[g][:, LANES:]
        for a in range(ATT_GROUP):
            hh = g * ATT_GROUP + a
            o_ref[:, hh * LANES:(hh + 1) * LANES] = out[a * tq:(a + 1) * tq].astype(o_ref.dtype)


def _dsa_attend(q, k, v, qi, ki, w):
    s = q.shape[0]
    tq = min(DSA_TQ, s)
    kb = min(DSA_KB, s)
    top_k = min(TOPK_MAX, s // 4)
    resident = lambda width: pl.BlockSpec((s, width), lambda i: (0, 0),
                                          pipeline_mode=pl.Buffered(1))
    tile = lambda width: pl.BlockSpec((tq, width), lambda i: (i, 0))
    return pl.pallas_call(
        functools.partial(_dsa_kernel, top_k=top_k),
        grid=(s // tq,),
        in_specs=[tile(q.shape[1]), tile(qi.shape[1]), tile(LANES),
                  resident(k.shape[1]), resident(v.shape[1]), resident(LANES)],
        out_specs=tile(q.shape[1]),
        out_shape=jax.ShapeDtypeStruct(q.shape, BF16),
        scratch_shapes=[
            pltpu.VMEM((s // kb, tq, kb), jnp.int32),
            pltpu.VMEM((2 * pl.cdiv(s // kb, 2), tq, kb), jnp.int16),
            pltpu.VMEM((2 * pl.cdiv(s // kb, 2), tq, kb), jnp.int16),
            pltpu.VMEM((IDX_HEADS, tq, LANES), BF16),
            pltpu.VMEM((IDX_HEADS, tq, LANES), F32),
            pltpu.VMEM((ATT_KV_HEADS, ATT_GROUP * tq, LANES), BF16),
            pltpu.VMEM((ATT_KV_HEADS, ATT_GROUP * tq, LANES), F32),
            pltpu.VMEM((ATT_KV_HEADS, ATT_GROUP * tq, 2 * LANES), F32),
        ],
        compiler_params=_cparams(1),
        name="dsa_attend",
    )(q, qi, w, k, v, ki)


def _dsa_layer(h, mix_norm, w_in, w_out):
    n_main = (ATT_HEADS + 2 * ATT_KV_HEADS) * ATT_HEAD_DIM + IDX_HEADS * IDX_DIM
    n_rest = w_in.shape[1] - n_main
    w_t = jnp.transpose(w_in)
    w_rest = jnp.pad(w_t[n_main:], ((0, LANES - n_rest), (0, 0)))
    qkv, kw = _matmul(h, w_t, norm_g=mix_norm, out_dtype=F32, n_cols=n_main,
                      w_rows_are_outputs=True, w_extra=w_rest, name="dsa_in")
    q, k, v, qi, ki, w = _dsa_rope(qkv, kw)
    o = _dsa_attend(q, k, v, qi, ki, w)
    return _matmul(o, w_out, residual=h, out_dtype=F32, name="dsa_out")


def _mlp_kernel(x_ref, g_ref, wu_ref, wd_ref, *rest, has_out_norm):
    og_ref = rest[0] if has_out_norm else None
    o_ref, xn_ref = rest[-2:]
    f = pl.program_id(1)

    @pl.when(f == 0)
    def _():
        xf = x_ref[...]
        ms = jnp.mean(xf * xf, axis=-1, keepdims=True)
        xn_ref[...] = (xf * lax.rsqrt(ms + RMS_EPS) * g_ref[...]).astype(BF16)
        o_ref[...] = xf

    u = jnp.maximum(_dot(xn_ref[...], wu_ref[...].astype(BF16)), 0.0)
    o_ref[...] += _dot((u * u).astype(BF16), wd_ref[...].astype(BF16))

    if has_out_norm:
        @pl.when(f == pl.num_programs(1) - 1)
        def _():
            of = o_ref[...]
            ms = jnp.mean(of * of, axis=-1, keepdims=True)
            o_ref[...] = of * lax.rsqrt(ms + RMS_EPS) * og_ref[...]


def _mlp(h, norm_g, w_up, w_down, out_norm_g=None, tm=1024, tf=512):
    m, d = h.shape
    dff = w_up.shape[1]
    tm = min(tm, m)
    tf = min(tf, dff)
    row = lambda v: v.reshape(1, d).astype(F32)
    gain_spec = pl.BlockSpec((1, d), lambda i, f: (0, 0))
    in_specs = [
        pl.BlockSpec((tm, d), lambda i, f: (i, 0), pipeline_mode=pl.Buffered(1)),
        gain_spec,
        pl.BlockSpec((d, tf), lambda i, f: (0, f)),
        pl.BlockSpec((tf, d), lambda i, f: (f, 0)),
    ]
    args = [h, row(norm_g), w_up, w_down]
    if out_norm_g is not None:
        in_specs.append(gain_spec)
        args.append(row(out_norm_g))
    return pl.pallas_call(
        functools.partial(_mlp_kernel, has_out_norm=out_norm_g is not None),
        grid=(m // tm, dff // tf),
        in_specs=in_specs,
        out_specs=pl.BlockSpec((tm, d), lambda i, f: (i, 0)),
        out_shape=jax.ShapeDtypeStruct((m, d), F32),
        scratch_shapes=[pltpu.VMEM((tm, d), BF16)],
        compiler_params=_cparams(2),
        name="mlp",
    )(*args)


def kernel(x, l0_mix_norm, l0_rg_in_w, l0_rg_conv_w, l0_rg_conv_b, l0_rg_wa, l0_rg_ba, l0_rg_wx, l0_rg_bx, l0_rg_lambda, l0_rg_out_w, l0_mlp_norm, l0_mlp_up, l0_mlp_down, l1_mix_norm, l1_dsa_in_w, l1_dsa_out_w, l1_mlp_norm, l1_mlp_up, l1_mlp_down, l2_mix_norm, l2_ssd_in_w, l2_ssd_conv_w, l2_ssd_conv_b, l2_ssd_dt_bias, l2_ssd_a_log, l2_ssd_d, l2_ssd_norm, l2_ssd_out_w, l2_mlp_norm, l2_mlp_up, l2_mlp_down, l3_mix_norm, l3_rg_in_w, l3_rg_conv_w, l3_rg_conv_b, l3_rg_wa, l3_rg_ba, l3_rg_wx, l3_rg_bx, l3_rg_lambda, l3_rg_out_w, l3_mlp_norm, l3_mlp_up, l3_mlp_down, final_norm):
    bsz, seq, d = x.shape
    outs = []
    for b in range(bsz):
        h = x[b]
        h = _rglru_layer(h, l0_mix_norm, l0_rg_in_w, l0_rg_conv_w, l0_rg_conv_b, l0_rg_wa,
                         l0_rg_ba, l0_rg_wx, l0_rg_bx, l0_rg_lambda, l0_rg_out_w)
        h = _mlp(h, l0_mlp_norm, l0_mlp_up, l0_mlp_down)
        h = _dsa_layer(h, l1_mix_norm, l1_dsa_in_w, l1_dsa_out_w)
        h = _mlp(h, l1_mlp_norm, l1_mlp_up, l1_mlp_down)
        h = _ssd_layer(h, l2_mix_norm, l2_ssd_in_w, l2_ssd_conv_w, l2_ssd_conv_b, l2_ssd_dt_bias,
                       l2_ssd_a_log, l2_ssd_d, l2_ssd_norm, l2_ssd_out_w)
        h = _mlp(h, l2_mlp_norm, l2_mlp_up, l2_mlp_down)
        h = _rglru_layer(h, l3_mix_norm, l3_rg_in_w, l3_rg_conv_w, l3_rg_conv_b, l3_rg_wa,
                         l3_rg_ba, l3_rg_wx, l3_rg_bx, l3_rg_lambda, l3_rg_out_w)
        outs.append(_mlp(h, l3_mlp_norm, l3_mlp_up, l3_mlp_down, out_norm_g=final_norm))
    return jnp.stack(outs, axis=0)
```

```python
import functools
import math

import jax
import jax.numpy as jnp
from jax import lax
from jax.experimental import pallas as pl
from jax.experimental.pallas import tpu as pltpu

F32 = jnp.float32
BF16 = jnp.bfloat16

RMS_EPS = 1e-6
ROPE_THETA = 10000.0
CHUNK = 64
RG_BLOCKS = 8
RG_CONV = 4
RG_C = 8.0
ATT_HEADS = 16
ATT_KV_HEADS = 4
ATT_HEAD_DIM = 128
ATT_GROUP = ATT_HEADS // ATT_KV_HEADS
IDX_HEADS = 16
IDX_DIM = 64
TOPK_MAX = 256
SSD_HEAD_DIM = 64
SSD_GROUPS = 8
SSD_STATE = 128
SSD_CONV = 4

V7X_VMEM_BYTES = 64 * 1024 * 1024
VMEM_LIMIT_BYTES = V7X_VMEM_BYTES - 8 * 1024 * 1024
LANES = 128
SUBLANES = 8

NEG_INF = float("-inf")
KEY_NEG_INF = -2139095041
NEG_INF_BITS = -8388608


def _cparams(n_axes):
    return pltpu.CompilerParams(
        dimension_semantics=("arbitrary",) * n_axes,
        vmem_limit_bytes=VMEM_LIMIT_BYTES,
    )


def _dot(a, b):
    return jnp.dot(a, b, preferred_element_type=F32)


def _dot_nt(a, b):
    return lax.dot_general(a, b, (((1,), (1,)), ((), ())), preferred_element_type=F32)


def _dot_tn(a, b):
    return lax.dot_general(a, b, (((0,), (0,)), ((), ())), preferred_element_type=F32)


def _softplus(x):
    return jnp.maximum(x, 0.0) + jnp.log1p(jnp.exp(-jnp.abs(x)))


def _sigmoid(x):
    return jax.nn.sigmoid(x)


def _mm_kernel(*refs, has_norm, has_res, has_extra, w_rows_are_outputs):
    it = iter(refs)
    x_ref = next(it)
    g_ref = next(it) if has_norm else None
    w_ref = next(it)
    r_ref = next(it) if has_res else None
    we_ref = next(it) if has_extra else None
    o_ref = next(it)
    oe_ref = next(it) if has_extra else None
    xn_ref = next(it) if has_norm else None
    j = pl.program_id(1)

    if has_norm:
        @pl.when(j == 0)
        def _():
            xf = x_ref[...].astype(F32)
            ms = jnp.mean(xf * xf, axis=-1, keepdims=True)
            xn_ref[...] = (xf * lax.rsqrt(ms + RMS_EPS) * g_ref[...]).astype(BF16)

        lhs_ref = xn_ref
    else:
        lhs_ref = x_ref

    if has_extra:
        @pl.when(j == 0)
        def _():
            oe_ref[...] = _dot_nt(lhs_ref[...].astype(BF16), we_ref[...].astype(BF16))

    wt = w_ref[...].astype(BF16)
    lhs = lhs_ref[...].astype(BF16)
    v = _dot_nt(lhs, wt) if w_rows_are_outputs else _dot(lhs, wt)
    if has_res:
        v = v + r_ref[...]
    o_ref[...] = v.astype(o_ref.dtype)


def _matmul(x, w, *, norm_g=None, residual=None, out_dtype=F32, tm=2048, tn=512,
            n_cols=None, w_rows_are_outputs=False, w_extra=None, name="matmul"):
    m, kdim = x.shape
    n_all = w.shape[0] if w_rows_are_outputs else w.shape[1]
    n = n_all if n_cols is None else n_cols
    tm = min(tm, m)
    tn = min(tn, n)
    has_norm = norm_g is not None
    assert m % tm == 0 and n % tn == 0
    grid = (m // tm, n // tn)

    x_mode = {"pipeline_mode": pl.Buffered(1)} if x.dtype == F32 else {}
    in_specs = [pl.BlockSpec((tm, kdim), lambda i, j: (i, 0), **x_mode)]
    args = [x]
    if has_norm:
        in_specs.append(pl.BlockSpec((1, kdim), lambda i, j: (0, 0)))
        args.append(norm_g.reshape(1, kdim).astype(F32))
    if w_rows_are_outputs:
        in_specs.append(pl.BlockSpec((tn, kdim), lambda i, j: (j, 0)))
    else:
        in_specs.append(pl.BlockSpec((kdim, tn), lambda i, j: (0, j)))
    args.append(w)
    if residual is not None:
        in_specs.append(pl.BlockSpec((tm, tn), lambda i, j: (i, j)))
        args.append(residual)
    out_specs = pl.BlockSpec((tm, tn), lambda i, j: (i, j))
    out_shape = jax.ShapeDtypeStruct((m, n), out_dtype)
    if w_extra is not None:
        in_specs.append(pl.BlockSpec((LANES, kdim), lambda i, j: (0, 0)))
        args.append(w_extra)
        out_specs = [out_specs, pl.BlockSpec((tm, LANES), lambda i, j: (i, 0))]
        out_shape = [out_shape, jax.ShapeDtypeStruct((m, LANES), F32)]

    return pl.pallas_call(
        functools.partial(_mm_kernel, has_norm=has_norm, has_res=residual is not None,
                          has_extra=w_extra is not None,
                          w_rows_are_outputs=w_rows_are_outputs),
        grid=grid,
        in_specs=in_specs,
        out_specs=out_specs,
        out_shape=out_shape,
        scratch_shapes=[pltpu.VMEM((tm, kdim), BF16)] if has_norm else [],
        compiler_params=_cparams(2),
        name=name,
    )(*args)


HALO = SUBLANES


def _causal_conv(buf, x_ref, cw_ref, cb_ref, n_taps):
    t_rows = x_ref.shape[0]
    buf[HALO:HALO + t_rows, :] = x_ref[...]
    ext = buf[...]
    acc = cw_ref[0:1, :] * ext
    for k in range(1, n_taps):
        acc = pltpu.roll(acc, 1, 0) + cw_ref[k:k + 1, :] * ext
    buf[0:HALO, :] = buf[t_rows:t_rows + HALO, :]
    return cb_ref[...] + acc[HALO:, :]


def _rg_kernel(gate_ref, x_ref, cw_ref, cb_ref, wa_ref, wx_ref, ba_ref, bx_ref,
               lam_ref, wo_ref, h_ref, o_ref, xbuf, a_scr, b_scr, hcar, y_prev):
    t = pl.program_id(0)
    t_rows, width = x_ref.shape
    n_blocks, bw, _ = wa_ref.shape

    @pl.when(t == 0)
    def _():
        xbuf[0:HALO, :] = jnp.zeros((HALO, width), F32)
        hcar[...] = jnp.zeros_like(hcar)
        y_prev[...] = jnp.zeros_like(y_prev)

    xc = _causal_conv(xbuf, x_ref, cw_ref, cb_ref, RG_CONV)
    xcb = xc.astype(BF16)
    sp = _softplus(-lam_ref[...])
    yp = y_prev[...]
    ow = o_ref.shape[1] // n_blocks
    for c in range(n_blocks):
        osl = slice(c * ow, (c + 1) * ow)
        o_ref[:, osl] = h_ref[:, osl] + _dot(yp, wo_ref[:, osl])
        sl = slice(c * bw, (c + 1) * bw)
        xs = xcb[:, sl]
        r = _sigmoid(_dot(xs, wa_ref[c]) + ba_ref[:, sl])
        i = _sigmoid(_dot(xs, wx_ref[c]) + bx_ref[:, sl])
        log_a = -RG_C * r * sp[:, sl]
        th = jnp.tanh(log_a)
        mult = jnp.sqrt(-2.0 * th / (1.0 - th))
        a_scr[:, sl] = jnp.exp(log_a)
        b_scr[:, sl] = mult * (i * xc[:, sl])

    def step(r, h):
        h = a_scr[pl.ds(r, 1), :] * h + b_scr[pl.ds(r, 1), :]
        b_scr[pl.ds(r, 1), :] = h
        return h

    hcar[...] = lax.fori_loop(0, t_rows, step, hcar[...], unroll=8)
    y_prev[...] = (b_scr[...] * jax.nn.gelu(gate_ref[...])).astype(y_prev.dtype)


def _rg_core_out(xg, h, conv_w, conv_b, wa, ba, wx, bx, lam, w_out, t_rows=256):
    s, two_w = xg.shape
    width = two_w // 2
    d = w_out.shape[1]
    t_rows = min(t_rows, s)
    nt = s // t_rows
    row = lambda v: v.reshape(1, width).astype(F32)
    full2 = lambda shape: pl.BlockSpec(shape, lambda t: (0, 0))
    full3 = lambda shape: pl.BlockSpec(shape, lambda t: (0, 0, 0))
    cur = lambda t: jnp.minimum(t, nt - 1)
    prev = lambda t: jnp.maximum(t - 1, 0)
    return pl.pallas_call(
        _rg_kernel,
        grid=(nt + 1,),
        in_specs=[
            pl.BlockSpec((t_rows, width), lambda t: (cur(t), 0)),
            pl.BlockSpec((t_rows, width), lambda t: (cur(t), 1)),
            full2((RG_CONV, width)), full2((1, width)),
            full3(wa.shape), full3(wx.shape),
            full2((1, width)), full2((1, width)), full2((1, width)),
            pl.BlockSpec((width, d), lambda t: (0, 0), pipeline_mode=pl.Buffered(1)),
            pl.BlockSpec((t_rows, d), lambda t: (prev(t), 0)),
        ],
        out_specs=pl.BlockSpec((t_rows, d), lambda t: (prev(t), 0)),
        out_shape=jax.ShapeDtypeStruct((s, d), F32),
        scratch_shapes=[
            pltpu.VMEM((t_rows + HALO, width), F32),
            pltpu.VMEM((t_rows, width), F32),
            pltpu.VMEM((t_rows, width), F32),
            pltpu.VMEM((1, width), F32),
            pltpu.VMEM((t_rows, width), BF16),
        ],
        compiler_params=_cparams(1),
        name="rg_core_out",
    )(xg, xg, conv_w.astype(F32), row(conv_b), wa.astype(BF16), wx.astype(BF16),
      row(ba), row(bx), row(lam), w_out.astype(BF16), h)


def _rglru_layer(h, mix_norm, w_in, conv_w, conv_b, w_a, b_a, w_x, b_x, lam, w_out):
    xg = _matmul(h, w_in, norm_g=mix_norm, out_dtype=F32, name="rg_in")
    return _rg_core_out(xg, h, conv_w, conv_b, w_a, b_a, w_x, b_x, lam, w_out)


SSD_TILE = 128
SSD_HPG = 8
SSD_GW = SSD_HPG * SSD_HEAD_DIM
SSD_GPS = 8


def _cumsum(v, axis):
    n = v.shape[axis]
    idx = lax.broadcasted_iota(jnp.int32, v.shape, axis)
    shift = 1
    while shift < n:
        v = v + jnp.where(idx >= shift, pltpu.roll(v, shift, axis), 0.0)
        shift *= 2
    return v


def _expand_heads(cols, base):
    rows = cols.shape[0]
    lane = lax.broadcasted_iota(jnp.int32, (rows, LANES), 1)
    parts = []
    for v in range(SSD_GW // LANES):
        lo = jnp.broadcast_to(cols[:, base + 2 * v:base + 2 * v + 1], (rows, LANES))
        hi = jnp.broadcast_to(cols[:, base + 2 * v + 1:base + 2 * v + 2], (rows, LANES))
        parts.append(jnp.where(lane < SSD_HEAD_DIM, lo, hi))
    return jnp.concatenate(parts, axis=1)


def _ssd_kernel(z_ref, x_ref, b_ref, c_ref, cwx_ref, cwb_ref, cwc_ref, cbx_ref,
                cbb_ref, cbc_ref, dtr_ref, dbr_ref, alr_ref, dsk_ref, ng_ref,
                wo_ref, h_ref, o_ref, xbuf, bbuf, cbuf, st, y_prev):
    t = pl.program_id(1)
    rows = x_ref.shape[0]
    hpg, gw, ns = SSD_HPG, SSD_GW, SSD_STATE

    @pl.when(t == 0)
    def _():
        xbuf[0:HALO, :] = jnp.zeros((HALO, xbuf.shape[1]), F32)
        bbuf[0:HALO, :] = jnp.zeros((HALO, bbuf.shape[1]), F32)
        cbuf[0:HALO, :] = jnp.zeros((HALO, cbuf.shape[1]), F32)
        st[...] = jnp.zeros_like(st)
        y_prev[...] = jnp.zeros_like(y_prev)

    def conv_silu(buf, ref, cw, cb):
        v = _causal_conv(buf, ref, cw, cb, SSD_CONV)
        return v * _sigmoid(v)

    xs_all = conv_silu(xbuf, x_ref, cwx_ref, cbx_ref)
    bm_all = conv_silu(bbuf, b_ref, cwb_ref, cbb_ref)
    cm_all = conv_silu(cbuf, c_ref, cwc_ref, cbc_ref)

    li = lax.broadcasted_iota(jnp.int32, (rows, rows), 0)
    si = lax.broadcasted_iota(jnp.int32, (rows, rows), 1)
    causal = li >= si
    eye = jnp.where(li == si, 1.0, 0.0)
    lane = lax.broadcasted_iota(jnp.int32, (rows, LANES), 1)

    groups = range(x_ref.shape[1] // gw)
    gsl = [slice(gi * gw, (gi + 1) * gw) for gi in groups]
    xs = [xs_all[:, gsl[gi]] for gi in groups]
    bmb = [bm_all[:, gi * ns:(gi + 1) * ns].astype(BF16) for gi in groups]
    cmb = [cm_all[:, gi * ns:(gi + 1) * ns].astype(BF16) for gi in groups]
    cb = [_dot_nt(cmb[gi], bmb[gi]) for gi in groups]
    stg = [st[:, gsl[gi]] for gi in groups]
    y_off = [_dot(cmb[gi], stg[gi].astype(BF16)) for gi in groups]

    csr, both = [], []
    for gi in groups:
        hs = slice(gi * hpg, (gi + 1) * hpg)
        dtr = _softplus(dtr_ref[hs, :] + dbr_ref[hs, :])
        csr.append(_cumsum(dtr * (-jnp.exp(alr_ref[hs, :])), 1))
        both.append(lax.dot_general(eye, jnp.concatenate([dtr, csr[gi]], axis=0),
                                    (((1,), (1,)), ((), ())),
                                    precision=lax.Precision.HIGHEST,
                                    preferred_element_type=F32))
    last = [both[gi][rows - 1:rows, :] for gi in groups]

    xdt = [xs[gi] * _expand_heads(both[gi], 0) for gi in groups]
    xdtb = [xdt[gi].astype(BF16) for gi in groups]
    xw = [(xdt[gi] * _expand_heads(jnp.exp(last[gi] - both[gi]), hpg)).astype(BF16)
          for gi in groups]
    st_new = [_dot_tn(bmb[gi], xw[gi]) for gi in groups]

    ys = [[] for _ in groups]
    for v in range(gw // LANES):
        pair = [[] for _ in groups]
        for j in (2 * v, 2 * v + 1):
            for gi in groups:
                seg = both[gi][:, hpg + j:hpg + j + 1] - csr[gi][j:j + 1, :]
                dec = jnp.exp(jnp.where(causal, seg, NEG_INF))
                pair[gi].append(_dot((cb[gi] * dec).astype(BF16),
                                     xdtb[gi][:, v * LANES:(v + 1) * LANES]))
        for gi in groups:
            ys[gi].append(jnp.where(lane < SSD_HEAD_DIM, pair[gi][0], pair[gi][1]))

    yp = y_prev[...]
    ow = o_ref.shape[1] // len(groups)
    for gi in groups:
        osl = slice(gi * ow, (gi + 1) * ow)
        o_ref[:, osl] = h_ref[:, osl] + _dot(yp, wo_ref[:, osl])
        y = jnp.concatenate(ys[gi], axis=1)
        y = y + y_off[gi] * _expand_heads(jnp.exp(both[gi]), hpg)
        y = y + xs[gi] * dsk_ref[:, gsl[gi]]
        z = z_ref[:, gsl[gi]]
        yz = y * (z * _sigmoid(z))
        yz = yz * lax.rsqrt(jnp.mean(yz * yz, axis=-1, keepdims=True) + RMS_EPS)
        y_prev[:, gsl[gi]] = (yz * ng_ref[:, gsl[gi]]).astype(y_prev.dtype)
        st[:, gsl[gi]] = stg[gi] * _expand_heads(jnp.exp(last[gi]), hpg) + st_new[gi]


def _ssd_core_out(zx, dt_raw, h, conv_w, conv_b, dt_bias, a_log, d_skip, norm_g, w_out):
    s = zx.shape[0]
    g = SSD_GROUPS
    gps = SSD_GPS
    assert gps == g
    inner = g * SSD_GW
    d = w_out.shape[1]
    rows = min(SSD_TILE, s)
    nt = s // rows
    cur = lambda t: jnp.minimum(t, nt - 1)
    prev = lambda t: jnp.maximum(t - 1, 0)
    xw, bw = gps * SSD_GW, gps * SSD_STATE
    nx = inner // xw
    nb = 2 * inner // bw
    nc = nb + g * SSD_STATE // bw
    cwb0 = inner // bw
    cwc0 = cwb0 + g * SSD_STATE // bw
    hps = gps * SSD_HPG

    dt_t = jnp.transpose(dt_raw)
    colv = lambda v: jnp.pad(v.astype(F32), (0, LANES - v.shape[0])).reshape(LANES, 1)
    conv_w = conv_w.astype(F32)
    conv_b = conv_b.reshape(1, -1).astype(F32)
    dsk = jnp.repeat(d_skip.astype(F32), SSD_HEAD_DIM).reshape(1, inner)
    ng = norm_g.reshape(1, inner).astype(F32)

    gs = lambda shape, fn: pl.BlockSpec(shape, fn)
    in_specs = [
        gs((rows, xw), lambda gi, t: (cur(t), gi)),
        gs((rows, xw), lambda gi, t: (cur(t), nx + gi)),
        gs((rows, bw), lambda gi, t: (cur(t), nb + gi)),
        gs((rows, bw), lambda gi, t: (cur(t), nc + gi)),
        gs((SSD_CONV, xw), lambda gi, t: (0, gi)),
        gs((SSD_CONV, bw), lambda gi, t: (0, cwb0 + gi)),
        gs((SSD_CONV, bw), lambda gi, t: (0, cwc0 + gi)),
        gs((1, xw), lambda gi, t: (0, gi)),
        gs((1, bw), lambda gi, t: (0, cwb0 + gi)),
        gs((1, bw), lambda gi, t: (0, cwc0 + gi)),
        gs((hps, rows), lambda gi, t: (gi, cur(t))),
        gs((hps, 1), lambda gi, t: (gi, 0)),
        gs((hps, 1), lambda gi, t: (gi, 0)),
        gs((1, xw), lambda gi, t: (0, gi)),
        gs((1, xw), lambda gi, t: (0, gi)),
        pl.BlockSpec((inner, d), lambda gi, t: (0, 0), pipeline_mode=pl.Buffered(1)),
        gs((rows, d), lambda gi, t: (prev(t), 0)),
    ]
    return pl.pallas_call(
        _ssd_kernel,
        grid=(g // gps, nt + 1),
        in_specs=in_specs,
        out_specs=pl.BlockSpec((rows, d), lambda gi, t: (prev(t), 0)),
        out_shape=jax.ShapeDtypeStruct((s, d), F32),
        scratch_shapes=[
            pltpu.VMEM((rows + HALO, xw), F32),
            pltpu.VMEM((rows + HALO, bw), F32),
            pltpu.VMEM((rows + HALO, bw), F32),
            pltpu.VMEM((SSD_STATE, xw), F32),
            pltpu.VMEM((rows, inner), BF16),
        ],
        compiler_params=_cparams(2),
        name="ssd_core_out",
    )(zx, zx, zx, zx, conv_w, conv_w, conv_w, conv_b, conv_b, conv_b,
      dt_t, colv(dt_bias), colv(a_log), dsk, ng, w_out.astype(BF16), h)


def _ssd_layer(h, mix_norm, w_in, conv_w, conv_b, dt_bias, a_log, d_skip, norm_g, w_out):
    heads = dt_bias.shape[0]
    n_zx = w_in.shape[1] - heads
    w_t = jnp.transpose(w_in)
    w_dt = jnp.pad(w_t[n_zx:], ((0, LANES - heads), (0, 0)))
    zx, dt_raw = _matmul(h, w_t, norm_g=mix_norm, out_dtype=F32, n_cols=n_zx,
                         w_rows_are_outputs=True, w_extra=w_dt, name="ssd_in")
    return _ssd_core_out(zx, dt_raw, h, conv_w, conv_b, dt_bias, a_log, d_skip, norm_g, w_out)


def _rope_kernel(qkv_ref, kw_ref, cos_ref, sin_ref, cosi_ref, sini_ref,
                 q_ref, k_ref, v_ref, qi_ref, ki_ref, w_ref, *, q_scale, w_scale):
    rows = qkv_ref.shape[0]
    lane = lax.broadcasted_iota(jnp.int32, (rows, LANES), 1)
    cos, sin = cos_ref[...], sin_ref[...]
    cosi, sini = cosi_ref[...], sini_ref[...]
    half_i = IDX_DIM // 2
    lower_i = (lane & (IDX_DIM - 1)) < half_i

    def rope_head(x):
        return x * cos + pltpu.roll(x, ATT_HEAD_DIM // 2, 1) * sin

    def rope_idx(x):
        rot = jnp.where(lower_i, pltpu.roll(x, LANES - half_i, 1), pltpu.roll(x, half_i, 1))
        return x * cosi + rot * sini

    nq = q_ref.shape[1] // LANES
    nkv = k_ref.shape[1] // LANES
    nqi = qi_ref.shape[1] // LANES
    for hh in range(nq):
        x = qkv_ref[:, hh * LANES:(hh + 1) * LANES]
        q_ref[:, hh * LANES:(hh + 1) * LANES] = (rope_head(x) * q_scale).astype(q_ref.dtype)
    off = nq
    for hh in range(nkv):
        x = qkv_ref[:, (off + hh) * LANES:(off + hh + 1) * LANES]
        k_ref[:, hh * LANES:(hh + 1) * LANES] = rope_head(x).astype(k_ref.dtype)
    off += nkv
    v_ref[...] = qkv_ref[:, off * LANES:(off + nkv) * LANES].astype(v_ref.dtype)
    off += nkv
    for hh in range(nqi):
        x = qkv_ref[:, (off + hh) * LANES:(off + hh + 1) * LANES]
        qi_ref[:, hh * LANES:(hh + 1) * LANES] = rope_idx(x).astype(qi_ref.dtype)
    kw = kw_ref[...]
    swapped = pltpu.roll(kw, LANES // 2, 1)
    ki_ref[...] = rope_idx(jnp.where(lane < IDX_DIM, kw, swapped)).astype(ki_ref.dtype)
    w_ref[...] = jnp.where(lane < IDX_HEADS, swapped * w_scale, 0.0)


def _rope_tables(seq):
    def table(dim):
        inv = ROPE_THETA ** (-jnp.arange(0, dim, 2, dtype=F32) / dim)
        ang = jnp.arange(seq, dtype=F32)[:, None] * inv[None, :]
        return jnp.cos(ang), jnp.sin(ang)

    c, s = table(ATT_HEAD_DIM)
    ci, si = table(IDX_DIM)
    return (jnp.concatenate([c, c], axis=1), jnp.concatenate([-s, s], axis=1),
            jnp.concatenate([ci, ci, ci, ci], axis=1), jnp.concatenate([-si, si, -si, si], axis=1))


def _dsa_rope(qkv, kw, t_rows=256):
    s = qkv.shape[0]
    t_rows = min(t_rows, s)
    dq = ATT_HEADS * ATT_HEAD_DIM
    dkv = ATT_KV_HEADS * ATT_HEAD_DIM
    dqi = IDX_HEADS * IDX_DIM
    cos, sin, cosi, sini = _rope_tables(s)
    tile = lambda width: pl.BlockSpec((t_rows, width), lambda t: (t, 0))
    return pl.pallas_call(
        functools.partial(_rope_kernel, q_scale=ATT_HEAD_DIM ** -0.5 * math.log2(math.e),
                          w_scale=IDX_HEADS ** -0.5 * IDX_DIM ** -0.5),
        grid=(s // t_rows,),
        in_specs=[tile(qkv.shape[1]), tile(LANES), tile(LANES), tile(LANES), tile(LANES), tile(LANES)],
        out_specs=[tile(dq), tile(dkv), tile(dkv), tile(dqi), tile(LANES), tile(LANES)],
        out_shape=[
            jax.ShapeDtypeStruct((s, dq), BF16),
            jax.ShapeDtypeStruct((s, dkv), BF16),
            jax.ShapeDtypeStruct((s, dkv), BF16),
            jax.ShapeDtypeStruct((s, dqi), BF16),
            jax.ShapeDtypeStruct((s, LANES), BF16),
            jax.ShapeDtypeStruct((s, LANES), F32),
        ],
        compiler_params=_cparams(1),
        name="dsa_rope",
    )(qkv, kw, cos, sin, cosi, sini)


DSA_TQ = 128
DSA_KB = 512


def _sortable_key(score):
    bits = lax.bitcast_convert_type(score, jnp.int32)
    return jnp.where(bits < 0, bits ^ jnp.int32(0x7FFFFFFF), bits)


def _key_hi(key):
    return jnp.right_shift(key, 16).astype(jnp.int16)


def _key_lo(key):
    return ((key & 0xFFFF) - 32768).astype(jnp.int16)


def _tile_lanes(v, width):
    return jnp.concatenate([v] * (width // v.shape[1]), axis=1)


def _dsa_kernel(q_ref, qi_ref, w_ref, k_ref, v_ref, ki_ref, o_ref,
                sc, hi16, lo16, qa, wb, qs, m_scr, acc_scr, *, top_k):
    i = pl.program_id(0)
    tq = q_ref.shape[0]
    kb_size = sc.shape[2]
    nkb = ((i + 1) * tq + kb_size - 1) // kb_size
    lane = lax.broadcasted_iota(jnp.int32, (tq, LANES), 1)
    chunk_shift = CHUNK.bit_length() - 1
    q_chunk = (i * tq + lax.broadcasted_iota(jnp.int32, (tq, kb_size), 0)) >> chunk_shift
    k_lane = lax.broadcasted_iota(jnp.int32, (tq, kb_size), 1)

    def admissible(kb):
        return ((kb * kb_size + k_lane) >> chunk_shift) <= q_chunk

    for h in range(IDX_HEADS):
        pair = qi_ref[:, (h // 2) * LANES:(h // 2 + 1) * LANES]
        keep = (lane < IDX_DIM) if h % 2 == 0 else (lane >= IDX_DIM)
        qa[h] = jnp.where(keep, pair, jnp.zeros_like(pair))
        wb[h] = jnp.broadcast_to(w_ref[:, h:h + 1], (tq, LANES))

    def score_block(kb):
        kis = ki_ref[pl.ds(pl.multiple_of(kb * kb_size, kb_size), kb_size), :]
        score = jnp.zeros((tq, kb_size), F32)
        for h in range(IDX_HEADS):
            lg = _dot_nt(qa[h], kis)
            score = score + jnp.maximum(lg, 0.0) * _tile_lanes(wb[h], kb_size)
        key = jnp.where(admissible(kb), _sortable_key(score), jnp.int32(KEY_NEG_INF))
        sc[kb] = key
        hi16[kb] = _key_hi(key)
        lo16[kb] = _key_lo(key)

    def score_pair(pi, carry):
        score_block(2 * pi)
        score_block(2 * pi + 1)
        return carry

    lax.fori_loop(0, nkb // 2, score_pair, 0)

    @pl.when(nkb % 2 == 1)
    def _():
        score_block(nkb - 1)

    @pl.when(nkb % 2 == 1)
    def _():
        pad = jnp.full((tq, kb_size), KEY_NEG_INF, jnp.int32)
        hi16[nkb] = _key_hi(pad)
        lo16[nkb] = _key_lo(pad)

    n_pairs = (nkb + 1) // 2
    i16 = jnp.int16

    def count16(arr, cand, strict):
        cand_t = _tile_lanes(cand.astype(i16), kb_size)

        def count_pair(pi, acc):
            for u in range(2):
                blk = arr[2 * pi + u]
                hit = (blk > cand_t) if strict else (blk >= cand_t)
                one = jnp.where(hit, jnp.ones_like(blk), jnp.zeros_like(blk))
                for c in range(kb_size // LANES):
                    acc = acc + one[:, c * LANES:(c + 1) * LANES]
            return acc

        acc = lax.fori_loop(0, n_pairs, count_pair, jnp.zeros((tq, LANES), i16))
        cnt = jnp.sum(acc.astype(jnp.int32).astype(F32), axis=1, keepdims=True)
        return jnp.broadcast_to(cnt, (tq, LANES))

    def kth_largest16(arr, target):
        def bit_step(b, thr):
            cand = thr + jnp.left_shift(jnp.int32(1), 15 - b)
            return jnp.where(count16(arr, cand, False) >= target, cand, thr)

        return lax.fori_loop(0, 16, bit_step, jnp.full((tq, LANES), -32768, jnp.int32))

    k_f = jnp.full((tq, LANES), float(top_k), F32)
    thr_hi = kth_largest16(hi16, k_f)
    need = k_f - count16(hi16, thr_hi, True)
    thr_hi_t = _tile_lanes(thr_hi.astype(i16), kb_size)

    def tie_block(kb, carry):
        lo16[kb] = jnp.where(hi16[kb] == thr_hi_t, lo16[kb], jnp.full((tq, kb_size), -32768, i16))
        return carry

    lax.fori_loop(0, 2 * n_pairs, tie_block, 0)
    thr_lo = kth_largest16(lo16, need)
    thr = jnp.left_shift(thr_hi, 16) | ((thr_lo + 32768) & 0xFFFF)
    thr_t = _tile_lanes(thr, kb_size)

    n_above = count16(lo16, thr_lo, True)
    need_eq = need - n_above
    n_eq = count16(lo16, thr_lo, False) - n_above
    real_thr = thr > KEY_NEG_INF
    extra = jnp.where(jnp.logical_and(real_thr, n_eq > need_eq), 1.0, 0.0)
    has_extra_ties = jnp.max(extra) > 0.0

    @pl.when(jnp.logical_not(has_extra_ties))
    def _():
        def bias_block(kb, carry):
            sel = jnp.logical_and(sc[kb] >= thr_t, admissible(kb))
            sc[kb] = jnp.where(sel, jnp.int32(0), jnp.int32(NEG_INF_BITS))
            return carry

        lax.fori_loop(0, nkb, bias_block, 0)

    @pl.when(has_extra_ties)
    def _():
        ki_ = lax.broadcasted_iota(jnp.int32, (kb_size, kb_size), 0)
        kj_ = lax.broadcasted_iota(jnp.int32, (kb_size, kb_size), 1)
        before = jnp.where(ki_ < kj_, 1.0, 0.0).astype(BF16)
        limit = _tile_lanes(jnp.where(real_thr, need_eq, float(kb_size * sc.shape[0])), kb_size)

        def bias_block(kb, seen):
            key = sc[kb]
            adm = admissible(kb)
            eq = jnp.logical_and(key == thr_t, adm)
            eq_b = jnp.where(eq, 1.0, 0.0).astype(BF16)
            rank = _dot(eq_b, before) + _tile_lanes(seen, kb_size)
            keep = jnp.logical_or(jnp.logical_and(key > thr_t, adm),
                                  jnp.logical_and(eq, rank < limit))
            sc[kb] = jnp.where(keep, jnp.int32(0), jnp.int32(NEG_INF_BITS))
            total = _dot(eq_b, jnp.ones((kb_size, LANES), BF16))
            return seen + total

        lax.fori_loop(0, nkb, bias_block, jnp.zeros((tq, LANES), F32))

    for g in range(ATT_KV_HEADS):
        qs[g] = jnp.concatenate(
            [q_ref[:, (g * ATT_GROUP + a) * LANES:(g * ATT_GROUP + a + 1) * LANES]
             for a in range(ATT_GROUP)], axis=0)
    m_scr[...] = jnp.full(m_scr.shape, NEG_INF, F32)
    acc_scr[...] = jnp.zeros_like(acc_scr)
    ones = jnp.ones((kb_size, LANES), BF16)

    def attn_block(kb):
        start = pl.multiple_of(kb * kb_size, kb_size)
        bias = lax.bitcast_convert_type(sc[kb], F32)
        bias = jnp.concatenate([bias] * ATT_GROUP, axis=0)
        for g in range(ATT_KV_HEADS):
            kblk = k_ref[pl.ds(start, kb_size), g * LANES:(g + 1) * LANES]
            vblk = v_ref[pl.ds(start, kb_size), g * LANES:(g + 1) * LANES]
            s = _dot_nt(qs[g], kblk) + bias
            m_old = m_scr[g]
            m_new = jnp.maximum(m_old, jnp.max(s, axis=1, keepdims=True))
            m_safe = jnp.where(m_new == NEG_INF, 0.0, m_new)
            alpha = jnp.exp2(m_old - m_safe)
            p = jnp.exp2(s - _tile_lanes(m_safe, kb_size))
            pv = _dot(p.astype(BF16), jnp.concatenate([vblk, ones], axis=1))
            acc_scr[g] = _tile_lanes(alpha, 2 * LANES) * acc_scr[g] + pv
            m_scr[g] = m_new

    def attn_quad(qi, carry):
        for u in range(4):
            attn_block(4 * qi + u)
        return carry

    lax.fori_loop(0, nkb // 4, attn_quad, 0)

    @pl.when(nkb % 4 >= 2)
    def _():
        first = (nkb // 4) * 4
        attn_block(first)
        attn_block(first + 1)

    @pl.when(nkb % 2 == 1)
    def _():
        attn_block(nkb - 1)

    for g in range(ATT_KV_HEADS):
        out = acc_scr[g][:, :LANES] / acc_scr[g][:, LANES:]
        for a in range(ATT_GROUP):
            hh = g * ATT_GROUP + a
            o_ref[:, hh * LANES:(hh + 1) * LANES] = out[a * tq:(a + 1) * tq].astype(o_ref.dtype)


def _dsa_attend(q, k, v, qi, ki, w):
    s = q.shape[0]
    tq = min(DSA_TQ, s)
    kb = min(DSA_KB, s)
    top_k = min(TOPK_MAX, s // 4)
    resident = lambda width: pl.BlockSpec((s, width), lambda i: (0, 0),
                                          pipeline_mode=pl.Buffered(1))
    tile = lambda width: pl.BlockSpec((tq, width), lambda i: (i, 0))
    return pl.pallas_call(
        functools.partial(_dsa_kernel, top_k=top_k),
        grid=(s // tq,),
        in_specs=[tile(q.shape[1]), tile(qi.shape[1]), tile(LANES),
                  resident(k.shape[1]), resident(v.shape[1]), resident(LANES)],
        out_specs=tile(q.shape[1]),
        out_shape=jax.ShapeDtypeStruct(q.shape, BF16),
        scratch_shapes=[
            pltpu.VMEM((s // kb, tq, kb), jnp.int32),
            pltpu.VMEM((2 * pl.cdiv(s // kb, 2), tq, kb), jnp.int16),
            pltpu.VMEM((2 * pl.cdiv(s // kb, 2), tq, kb), jnp.int16),
            pltpu.VMEM((IDX_HEADS, tq, LANES), BF16),
            pltpu.VMEM((IDX_HEADS, tq, LANES), F32),
            pltpu.VMEM((ATT_KV_HEADS, ATT_GROUP * tq, LANES), BF16),
            pltpu.VMEM((ATT_KV_HEADS, ATT_GROUP * tq, LANES), F32),
            pltpu.VMEM((ATT_KV_HEADS, ATT_GROUP * tq, 2 * LANES), F32),
        ],
        compiler_params=_cparams(1),
        name="dsa_attend",
    )(q, qi, w, k, v, ki)


def _dsa_layer(h, mix_norm, w_in, w_out):
    n_main = (ATT_HEADS + 2 * ATT_KV_HEADS) * ATT_HEAD_DIM + IDX_HEADS * IDX_DIM
    n_rest = w_in.shape[1] - n_main
    w_t = jnp.transpose(w_in)
    w_rest = jnp.pad(w_t[n_main:], ((0, LANES - n_rest), (0, 0)))
    qkv, kw = _matmul(h, w_t, norm_g=mix_norm, out_dtype=F32, n_cols=n_main,
                      w_rows_are_outputs=True, w_extra=w_rest, name="dsa_in")
    q, k, v, qi, ki, w = _dsa_rope(qkv, kw)
    o = _dsa_attend(q, k, v, qi, ki, w)
    return _matmul(o, w_out, residual=h, out_dtype=F32, name="dsa_out")


def _mlp_kernel(x_ref, g_ref, wu_ref, wd_ref, *rest, has_out_norm):
    og_ref = rest[0] if has_out_norm else None
    o_ref, xn_ref = rest[-2:]
    f = pl.program_id(1)

    @pl.when(f == 0)
    def _():
        xf = x_ref[...]
        ms = jnp.mean(xf * xf, axis=-1, keepdims=True)
        xn_ref[...] = (xf * lax.rsqrt(ms + RMS_EPS) * g_ref[...]).astype(BF16)
        o_ref[...] = xf

    u = jnp.maximum(_dot(xn_ref[...], wu_ref[...].astype(BF16)), 0.0)
    o_ref[...] += _dot((u * u).astype(BF16), wd_ref[...].astype(BF16))

    if has_out_norm:
        @pl.when(f == pl.num_programs(1) - 1)
        def _():
            of = o_ref[...]
            ms = jnp.mean(of * of, axis=-1, keepdims=True)
            o_ref[...] = of * lax.rsqrt(ms + RMS_EPS) * og_ref[...]


def _mlp(h, norm_g, w_up, w_down, out_norm_g=None, tm=1024, tf=512):
    m, d = h.shape
    dff = w_up.shape[1]
    tm = min(tm, m)
    tf = min(tf, dff)
    row = lambda v: v.reshape(1, d).astype(F32)
    gain_spec = pl.BlockSpec((1, d), lambda i, f: (0, 0))
    in_specs = [
        pl.BlockSpec((tm, d), lambda i, f: (i, 0), pipeline_mode=pl.Buffered(1)),
        gain_spec,
        pl.BlockSpec((d, tf), lambda i, f: (0, f)),
        pl.BlockSpec((tf, d), lambda i, f: (f, 0)),
    ]
    args = [h, row(norm_g), w_up, w_down]
    if out_norm_g is not None:
        in_specs.append(gain_spec)
        args.append(row(out_norm_g))
    return pl.pallas_call(
        functools.partial(_mlp_kernel, has_out_norm=out_norm_g is not None),
        grid=(m // tm, dff // tf),
        in_specs=in_specs,
        out_specs=pl.BlockSpec((tm, d), lambda i, f: (i, 0)),
        out_shape=jax.ShapeDtypeStruct((m, d), F32),
        scratch_shapes=[pltpu.VMEM((tm, d), BF16)],
        compiler_params=_cparams(2),
        name="mlp",
    )(*args)


def kernel(x, l0_mix_norm, l0_rg_in_w, l0_rg_conv_w, l0_rg_conv_b, l0_rg_wa, l0_rg_ba, l0_rg_wx, l0_rg_bx, l0_rg_lambda, l0_rg_out_w, l0_mlp_norm, l0_mlp_up, l0_mlp_down, l1_mix_norm, l1_dsa_in_w, l1_dsa_out_w, l1_mlp_norm, l1_mlp_up, l1_mlp_down, l2_mix_norm, l2_ssd_in_w, l2_ssd_conv_w, l2_ssd_conv_b, l2_ssd_dt_bias, l2_ssd_a_log, l2_ssd_d, l2_ssd_norm, l2_ssd_out_w, l2_mlp_norm, l2_mlp_up, l2_mlp_down, l3_mix_norm, l3_rg_in_w, l3_rg_conv_w, l3_rg_conv_b, l3_rg_wa, l3_rg_ba, l3_rg_wx, l3_rg_bx, l3_rg_lambda, l3_rg_out_w, l3_mlp_norm, l3_mlp_up, l3_mlp_down, final_norm):
    bsz, seq, d = x.shape
    outs = []
    for b in range(bsz):
        h = x[b]
        h = _rglru_layer(h, l0_mix_norm, l0_rg_in_w, l0_rg_conv_w, l0_rg_conv_b, l0_rg_wa,
                         l0_rg_ba, l0_rg_wx, l0_rg_bx, l0_rg_lambda, l0_rg_out_w)
        h = _mlp(h, l0_mlp_norm, l0_mlp_up, l0_mlp_down)
        h = _dsa_layer(h, l1_mix_norm, l1_dsa_in_w, l1_dsa_out_w)
        h = _mlp(h, l1_mlp_norm, l1_mlp_up, l1_mlp_down)
        h = _ssd_layer(h, l2_mix_norm, l2_ssd_in_w, l2_ssd_conv_w, l2_ssd_conv_b, l2_ssd_dt_bias,
                       l2_ssd_a_log, l2_ssd_d, l2_ssd_norm, l2_ssd_out_w)
        h = _mlp(h, l2_mlp_norm, l2_mlp_up, l2_mlp_down)
        h = _rglru_layer(h, l3_mix_norm, l3_rg_in_w, l3_rg_conv_w, l3_rg_conv_b, l3_rg_wa,
                         l3_rg_ba, l3_rg_wx, l3_rg_bx, l3_rg_lambda, l3_rg_out_w)
        outs.append(_mlp(h, l3_mlp_norm, l3_mlp_up, l3_mlp_down, out_norm_g=final_norm))
    return jnp.stack(outs, axis=0)
```
